```python
import jax, jax.numpy as jnp
from jax import lax
import numpy as np

D_MODEL = 2048
BATCH = 8
SEQ = 2048
DEPTH = 1
DEC_BATCH = 32
DEC_SEQ = 8
PAST_LEN = 16384
PAGE_SIZE = 128

RET_HEADS = 8
RET_DK = 128
RET_DV = 128
FOX_HEADS = 8
FOX_HD = 128
MEM_HEADS = 4
MEM_HD = 256
N_MEM = 256
N_BRANCH = 3
D_FF = 4 * D_MODEL
RET_CHUNK = 128
Q_BLOCK = 128
ROPE_BASE = 10000.0
RMS_EPS = 1e-6
GN_EPS = 1e-5
FORGET_BIAS_MIN = 2.0
FORGET_BIAS_MAX = 10.0
D_IN = 2 * RET_HEADS * RET_DK + 2 * RET_HEADS * RET_DV + 3 * FOX_HEADS * FOX_HD + FOX_HEADS + MEM_HEADS * MEM_HD + N_BRANCH * D_MODEL

kernel_name = "hybrid_retention_fox_memxattn_step"

F32 = jnp.float32


def rms_norm(x, g):
    xf = x.astype(F32)
    y = xf * lax.rsqrt(jnp.mean(xf * xf, axis=-1, keepdims=True) + RMS_EPS)
    return (y * g.astype(F32)).astype(x.dtype)


def rope(x, pos):
    half = x.shape[-1] // 2
    inv = ROPE_BASE ** (-jnp.arange(half, dtype=F32) / half)
    ang = pos.astype(F32)[:, None] * inv[None, :]
    cos = jnp.cos(ang)[None, :, None, :]
    sin = jnp.sin(ang)[None, :, None, :]
    xf = x.astype(F32)
    x1, x2 = xf[..., :half], xf[..., half:]
    return jnp.concatenate([x1 * cos - x2 * sin, x1 * sin + x2 * cos], axis=-1)


def in_projection(h, w_in, b_fox_f):
    B, L = h.shape[:2]
    sizes = [RET_HEADS * RET_DK, RET_HEADS * RET_DK, RET_HEADS * RET_DV, RET_HEADS * RET_DV,
             FOX_HEADS * FOX_HD, FOX_HEADS * FOX_HD, FOX_HEADS * FOX_HD, FOX_HEADS,
             MEM_HEADS * MEM_HD, N_BRANCH * D_MODEL]
    z = jnp.einsum('bld,de->ble', h, w_in)
    rq, rk, rv, rg, fq, fk, fv, ff, mq, gates = jnp.split(z, [int(c) for c in np.cumsum(sizes)[:-1]], axis=-1)
    rq = rq.reshape(B, L, RET_HEADS, RET_DK)
    rk = rk.reshape(B, L, RET_HEADS, RET_DK)
    rv = rv.reshape(B, L, RET_HEADS, RET_DV)
    fq = fq.reshape(B, L, FOX_HEADS, FOX_HD)
    fk = fk.reshape(B, L, FOX_HEADS, FOX_HD)
    fv = fv.reshape(B, L, FOX_HEADS, FOX_HD)
    logf = jax.nn.log_sigmoid(ff.astype(F32) + b_fox_f.astype(F32))
    mq = mq.reshape(B, L, MEM_HEADS, MEM_HD)
    return rq, rk, rv, rg, fq, fk, fv, logf, mq, gates


def retention_chunkwise(q, k, v, s0):
    B, L, H, dk = q.shape
    dv = v.shape[-1]
    C = RET_CHUNK if L % RET_CHUNK == 0 else L
    n = L // C
    lg = jnp.log1p(-jnp.exp2(-5.0 - jnp.arange(H, dtype=F32)))
    i = jnp.arange(C, dtype=F32)
    diff = i[:, None] - i[None, :]
    inner_decay = jnp.where((diff >= 0)[None], jnp.exp(jnp.maximum(diff, 0.0)[None] * lg[:, None, None]), 0.0)
    q_decay = jnp.exp((i[None, :] + 1.0) * lg[:, None])
    k_decay = jnp.exp((C - 1.0 - i)[None, :] * lg[:, None])
    chunk_decay = jnp.exp(C * lg)

    def to_chunks(t):
        return t.reshape(B, n, C, H, t.shape[-1]).transpose(1, 0, 3, 2, 4)

    def step(S, xs):
        qc, kc, vc = xs
        att = jnp.einsum('bhid,bhjd->bhij', qc, kc) * inner_decay[None]
        out = jnp.einsum('bhij,bhjv->bhiv', att, vc) + jnp.einsum('bhid,bhdv->bhiv', qc, S) * q_decay[None, :, :, None]
        S = S * chunk_decay[None, :, None, None] + jnp.einsum('bhjd,bhjv->bhdv', kc * k_decay[None, :, :, None], vc)
        return S, out

    S, outs = lax.scan(step, s0.astype(F32), (to_chunks(q), to_chunks(k), to_chunks(v)))
    return outs.transpose(1, 0, 3, 2, 4).reshape(B, L, H, dv), S


def head_group_norm(o, g):
    B, L, H, dv = o.shape
    mu = jnp.mean(o, axis=-1, keepdims=True)
    var = jnp.mean(jnp.square(o - mu), axis=-1, keepdims=True)
    y = (o - mu) * lax.rsqrt(var + GN_EPS)
    return y.reshape(B, L, H * dv) * g.astype(F32)


def retention_branch(rq, rk, rv, rg, pos, s0, gn_ret):
    q = rope(rq, pos)
    k = rope(rk, pos) * (RET_DK ** -0.5)
    o, S = retention_chunkwise(q, k, rv.astype(F32), s0)
    o = head_group_norm(o, gn_ret)
    return (jax.nn.silu(rg.astype(F32)) * o).astype(rg.dtype), S


def fox_prompt(q, k, v, logf):
    B, L, H, d = q.shape
    qf = q.astype(F32) * (d ** -0.5)
    kf = k.astype(F32)
    vf = v.astype(F32)
    Ct = jnp.cumsum(logf, axis=1).transpose(0, 2, 1)
    kpos = jnp.arange(L)

    def block(bi):
        start = bi * Q_BLOCK
        qb = lax.dynamic_slice_in_dim(qf, start, Q_BLOCK, axis=1)
        cb = lax.dynamic_slice_in_dim(Ct, start, Q_BLOCK, axis=2)
        qpos = start + jnp.arange(Q_BLOCK)
        logits = jnp.einsum('bqhd,bkhd->bhqk', qb, kf) + (cb[..., :, None] - Ct[..., None, :])
        logits = jnp.where((kpos[None, :] <= qpos[:, None])[None, None], logits, -jnp.inf)
        p = jax.nn.softmax(logits, axis=-1)
        return jnp.einsum('bhqk,bkhd->bqhd', p, vf)

    outs = lax.map(block, jnp.arange(L // Q_BLOCK))
    return outs.transpose(1, 0, 2, 3, 4).reshape(B, L, H * d).astype(q.dtype)


def fox_sample(q, k, v, logf, cache_k, cache_v, cache_logf, page_table, layer):
    B, T, H, d = q.shape
    n_pages = page_table.shape[1]
    ps = cache_k.shape[2]
    qf = q.astype(F32) * (d ** -0.5)
    kf = k.astype(F32)
    vf = v.astype(F32)
    Pt = jnp.cumsum(logf, axis=1).transpose(0, 2, 1)
    lf_past = cache_logf[layer, page_table].astype(F32).reshape(B, n_pages * ps, H)
    R = lax.cumsum(lf_past, axis=1, reverse=True) - lf_past
    R_pages = R.reshape(B, n_pages, ps, H).transpose(1, 0, 3, 2)
    tpos = jnp.arange(T)
    logits = jnp.einsum('bqhd,bkhd->bhqk', qf, kf) + (Pt[..., :, None] - Pt[..., None, :])
    logits = jnp.where((tpos[None, :] <= tpos[:, None])[None, None], logits, -jnp.inf)
    m = jnp.max(logits, axis=-1)
    p = jnp.exp(logits - m[..., None])
    l = jnp.sum(p, axis=-1)
    acc = jnp.einsum('bhqk,bkhd->bhqd', p, vf)

    def page_step(carry, xs):
        m, l, acc = carry
        ids, Rp = xs
        kp = cache_k[layer, ids].astype(F32)
        vp = cache_v[layer, ids].astype(F32)
        s = jnp.einsum('bqhd,bkhd->bhqk', qf, kp) + Pt[..., :, None] + Rp[..., None, :]
        m_new = jnp.maximum(m, jnp.max(s, axis=-1))
        corr = jnp.exp(m - m_new)
        pp = jnp.exp(s - m_new[..., None])
        l = l * corr + jnp.sum(pp, axis=-1)
        acc = acc * corr[..., None] + jnp.einsum('bhqk,bkhd->bhqd', pp, vp)
        return (m_new, l, acc), None

    (m, l, acc), _ = lax.scan(page_step, (m, l, acc), (page_table.T, R_pages))
    out = acc / l[..., None]
    return out.transpose(0, 2, 1, 3).reshape(B, T, H * d).astype(q.dtype)


def memory_kv(mem, norm_mem, w_mem_kv):
    B, M = mem.shape[:2]
    kv = jnp.einsum('bmd,de->bme', rms_norm(mem, norm_mem), w_mem_kv)
    mk, mv = jnp.split(kv, 2, axis=-1)
    return mk.reshape(B, M, MEM_HEADS, MEM_HD), mv.reshape(B, M, MEM_HEADS, MEM_HD)


def memory_attend(mq, mk, mv):
    B, L = mq.shape[:2]
    qf = mq.astype(F32) * (MEM_HD ** -0.5)
    logits = jnp.einsum('bqhd,bkhd->bhqk', qf, mk.astype(F32))
    p = jax.nn.softmax(logits, axis=-1)
    o = jnp.einsum('bhqk,bkhd->bqhd', p, mv.astype(F32))
    return o.reshape(B, L, MEM_HEADS * MEM_HD).astype(mq.dtype)


def merge_branches(o_ret, o_fox, o_mem, gate_logits, w_br_ret, w_br_fox, w_br_mem, w_o):
    B, L = o_ret.shape[:2]
    g = jax.nn.sigmoid(gate_logits.astype(F32)).reshape(B, L, N_BRANCH, D_MODEL)
    merged = (g[:, :, 0] * jnp.einsum('ble,ed->bld', o_ret, w_br_ret).astype(F32)
              + g[:, :, 1] * jnp.einsum('ble,ed->bld', o_fox, w_br_fox).astype(F32)
              + g[:, :, 2] * jnp.einsum('ble,ed->bld', o_mem, w_br_mem).astype(F32))
    return jnp.einsum('bld,de->ble', merged.astype(o_ret.dtype), w_o)


def sq_relu_mlp(x, norm_mlp, w_up, w_down):
    u = jax.nn.relu(jnp.einsum('bld,df->blf', rms_norm(x, norm_mlp), w_up))
    return jnp.einsum('blf,fd->bld', u * u, w_down)


def setup_inputs(seed: int = 0) -> dict:
    key = jax.random.key(seed)
    ks = jax.random.split(key, 26)
    n_pages = PAST_LEN // PAGE_SIZE
    n_used = DEC_BATCH * n_pages
    n_pool = n_used + max(1, n_used // 4)
    nrm = lambda k, s: jax.random.normal(k, s, dtype=F32)
    page_table = jax.random.permutation(ks[0], n_pool)[:n_used].reshape(DEC_BATCH, n_pages).astype(jnp.int32)
    fbias = jnp.linspace(FORGET_BIAS_MIN, FORGET_BIAS_MAX, FOX_HEADS, dtype=F32)
    return {
        "x_prompt": nrm(ks[1], (BATCH, SEQ, D_MODEL)),
        "x_sample": nrm(ks[2], (DEC_BATCH, DEC_SEQ, D_MODEL)),
        "mem_prompt": nrm(ks[3], (BATCH, N_MEM, D_MODEL)),
        "state_ret": 0.5 * nrm(ks[4], (DEPTH, DEC_BATCH, RET_HEADS, RET_DK, RET_DV)),
        "cache_fox_k": nrm(ks[5], (DEPTH, n_pool, PAGE_SIZE, FOX_HEADS, FOX_HD)),
        "cache_fox_v": nrm(ks[6], (DEPTH, n_pool, PAGE_SIZE, FOX_HEADS, FOX_HD)),
        "cache_fox_logf": jax.nn.log_sigmoid(fbias + nrm(ks[7], (DEPTH, n_pool, PAGE_SIZE, FOX_HEADS))),
        "cache_mem_k": nrm(ks[8], (DEPTH, DEC_BATCH, N_MEM, MEM_HEADS, MEM_HD)),
        "cache_mem_v": nrm(ks[9], (DEPTH, DEC_BATCH, N_MEM, MEM_HEADS, MEM_HD)),
        "page_table": page_table,
        "norm_mix": 1.0 + 0.02 * nrm(ks[10], (DEPTH, D_MODEL)),
        "w_in": nrm(ks[11], (DEPTH, D_MODEL, D_IN)) * D_MODEL ** -0.5,
        "b_fox_f": fbias + 0.5 * nrm(ks[12], (DEPTH, FOX_HEADS)),
        "gn_ret": 1.0 + 0.02 * nrm(ks[13], (DEPTH, RET_HEADS * RET_DV)),
        "norm_mem": 1.0 + 0.02 * nrm(ks[14], (DEPTH, D_MODEL)),
        "w_mem_kv": nrm(ks[15], (DEPTH, D_MODEL, 2 * MEM_HEADS * MEM_HD)) * D_MODEL ** -0.5,
        "w_br_ret": nrm(ks[16], (DEPTH, RET_HEADS * RET_DV, D_MODEL)) * (RET_HEADS * RET_DV) ** -0.5,
        "w_br_fox": nrm(ks[17], (DEPTH, FOX_HEADS * FOX_HD, D_MODEL)) * (FOX_HEADS * FOX_HD) ** -0.5,
        "w_br_mem": nrm(ks[18], (DEPTH, MEM_HEADS * MEM_HD, D_MODEL)) * (MEM_HEADS * MEM_HD) ** -0.5,
        "w_o": nrm(ks[19], (DEPTH, D_MODEL, D_MODEL)) * D_MODEL ** -0.5,
        "norm_mlp": 1.0 + 0.02 * nrm(ks[20], (DEPTH, D_MODEL)),
        "w_up": nrm(ks[21], (DEPTH, D_MODEL, D_FF)) * D_MODEL ** -0.5,
        "w_down": nrm(ks[22], (DEPTH, D_FF, D_MODEL)) * D_FF ** -0.5,
        "norm_final": 1.0 + 0.02 * nrm(ks[23], (D_MODEL,)),
    }


def reference(x_prompt, x_sample, mem_prompt, state_ret, cache_fox_k, cache_fox_v, cache_fox_logf,
              cache_mem_k, cache_mem_v, page_table, norm_mix, w_in, b_fox_f, gn_ret, norm_mem, w_mem_kv,
              w_br_ret, w_br_fox, w_br_mem, w_o, norm_mlp, w_up, w_down, norm_final):
    B, L = x_prompt.shape[:2]
    Bd, T = x_sample.shape[:2]
    past_len = page_table.shape[1] * cache_fox_k.shape[2]
    pos_p = jnp.arange(L, dtype=jnp.int32)
    pos_s = past_len + jnp.arange(T, dtype=jnp.int32)
    xp, xs = x_prompt, x_sample
    sp_l, ss_l = [], []
    kp_l, vp_l, fp_l = [], [], []
    ks_l, vs_l, fs_l = [], [], []
    mk_l, mv_l = [], []
    for l in range(DEPTH):
        hp = rms_norm(xp, norm_mix[l])
        rq, rk, rv, rg, fq, fk, fv, logf, mq, gates = in_projection(hp, w_in[l], b_fox_f[l])
        s0 = jnp.zeros((B, RET_HEADS, RET_DK, RET_DV), F32)
        o_ret, s_new = retention_branch(rq, rk, rv, rg, pos_p, s0, gn_ret[l])
        o_fox = fox_prompt(fq, fk, fv, logf)
        mk, mv = memory_kv(mem_prompt, norm_mem[l], w_mem_kv[l])
        o_mem = memory_attend(mq, mk, mv)
        xp = xp + merge_branches(o_ret, o_fox, o_mem, gates, w_br_ret[l], w_br_fox[l], w_br_mem[l], w_o[l])
        xp = xp + sq_relu_mlp(xp, norm_mlp[l], w_up[l], w_down[l])
        sp_l.append(s_new)
        kp_l.append(fk)
        vp_l.append(fv)
        fp_l.append(logf)
        mk_l.append(mk)
        mv_l.append(mv)
        hs = rms_norm(xs, norm_mix[l])
        rq, rk, rv, rg, fq, fk, fv, logf, mq, gates = in_projection(hs, w_in[l], b_fox_f[l])
        o_ret, s_new = retention_branch(rq, rk, rv, rg, pos_s, state_ret[l], gn_ret[l])
        o_fox = fox_sample(fq, fk, fv, logf, cache_fox_k, cache_fox_v, cache_fox_logf, page_table, l)
        o_mem = memory_attend(mq, cache_mem_k[l], cache_mem_v[l])
        xs = xs + merge_branches(o_ret, o_fox, o_mem, gates, w_br_ret[l], w_br_fox[l], w_br_mem[l], w_o[l])
        xs = xs + sq_relu_mlp(xs, norm_mlp[l], w_up[l], w_down[l])
        ss_l.append(s_new)
        ks_l.append(fk)
        vs_l.append(fv)
        fs_l.append(logf)
    y_prompt = rms_norm(xp, norm_final)
    y_sample = rms_norm(xs, norm_final)
    return (y_prompt, y_sample, jnp.stack(sp_l), jnp.stack(ss_l), jnp.stack(kp_l), jnp.stack(vp_l), jnp.stack(fp_l),
            jnp.stack(ks_l), jnp.stack(vs_l), jnp.stack(fs_l), jnp.stack(mk_l), jnp.stack(mv_l))
```

```python
import functools

import jax
import jax.numpy as jnp
import numpy as np
from jax import lax
from jax.experimental import pallas as pl
from jax.experimental.pallas import tpu as pltpu

F32 = jnp.float32
BF16 = jnp.bfloat16

RET_HEADS = 8
RET_DK = 128
RET_DV = 128
FOX_HEADS = 8
FOX_HD = 128
MEM_HEADS = 4
MEM_HD = 256
N_BRANCH = 3
RET_CHUNK = 128
ROPE_BASE = 10000.0
RMS_EPS = 1e-6
GN_EPS = 1e-5

HEAD_COLS = 1024
LANES = 128
VMEM_LIMIT = 56 * 1024 * 1024


def _dot(a, b):
    return jnp.dot(a, b, preferred_element_type=F32)


def _dot_nt(a, b):
    return lax.dot_general(a, b, (((1,), (1,)), ((), ())), preferred_element_type=F32)


def _dot_tn(a, b):
    return lax.dot_general(a, b, (((0,), (0,)), ((), ())), preferred_element_type=F32)


def _params(sem):
    return pltpu.CompilerParams(dimension_semantics=sem, vmem_limit_bytes=VMEM_LIMIT)


def _rms_normed(x, g):
    ms = jnp.mean(x * x, axis=-1, keepdims=True)
    return x * lax.rsqrt(ms + RMS_EPS) * g


def _log_sigmoid(x):
    return jnp.minimum(x, 0.0) - jnp.log1p(jnp.exp(-jnp.abs(x)))


def _norm_proj_kernel(*refs, seg_bounds, has_ff):
    x_ref, g_ref, w_ref = refs[:3]
    pos = 3
    if has_ff:
        wff_ref, bff_ref = refs[3:5]
        pos = 5
    n_seg = len(seg_bounds)
    out_refs = refs[pos:pos + n_seg]
    pos += n_seg
    if has_ff:
        logf_ref = refs[pos]
        pos += 1
    hn_ref = refs[pos]
    j = pl.program_id(1)

    @pl.when(j == 0)
    def _():
        hn = _rms_normed(x_ref[...], g_ref[...]).astype(BF16)
        hn_ref[...] = hn
        if has_ff:
            logf_ref[...] = _log_sigmoid(_dot(hn, wff_ref[...]) + bff_ref[...])

    acc = _dot(hn_ref[...], w_ref[...])
    for (start, n), o_ref in zip(seg_bounds, out_refs):
        @pl.when((j >= start) & (j < start + n))
        def _(o_ref=o_ref):
            o_ref[...] = acc.astype(o_ref.dtype)


def _norm_proj(x2d, g, w, segs, tm, tn, w_ff=None, b_ff=None):
    T, D = x2d.shape
    N = w.shape[1]
    has_ff = w_ff is not None
    bounds = []
    start = 0
    for n, _ in segs:
        bounds.append((start, n))
        start += n
    assert start * tn == N and T % tm == 0
    in_specs = [
        pl.BlockSpec((tm, D), lambda i, j: (i, 0)),
        pl.BlockSpec((1, D), lambda i, j: (0, 0)),
        pl.BlockSpec((D, tn), lambda i, j: (0, j)),
    ]
    args = [x2d, g.reshape(1, D), w]
    if has_ff:
        in_specs += [pl.BlockSpec((D, LANES), lambda i, j: (0, 0)),
                     pl.BlockSpec((1, LANES), lambda i, j: (0, 0))]
        args += [w_ff, b_ff]
    out_specs = []
    out_shape = []
    for (s, n), (_, dt) in zip(bounds, segs):
        out_specs.append(pl.BlockSpec(
            (tm, tn), lambda i, j, s=s, n=n: (i, jnp.clip(j - s, 0, n - 1))))
        out_shape.append(jax.ShapeDtypeStruct((T, n * tn), dt))
    if has_ff:
        out_specs.append(pl.BlockSpec((tm, LANES), lambda i, j: (i, 0)))
        out_shape.append(jax.ShapeDtypeStruct((T, LANES), F32))
    return pl.pallas_call(
        functools.partial(_norm_proj_kernel, seg_bounds=tuple(bounds), has_ff=has_ff),
        grid=(T // tm, N // tn),
        in_specs=in_specs,
        out_specs=out_specs,
        out_shape=out_shape,
        scratch_shapes=[pltpu.VMEM((tm, D), BF16)],
        compiler_params=_params(("parallel", "arbitrary")),
        name="norm_proj",
    )(*args)


def _retention_kernel(*refs, T, Cp, has_s0):
    q_ref, k_ref, v_ref, g_ref, cos_ref, sin_ref, inner_ref, qd_ref, kd_ref, cd_ref, gn_ref = refs[:11]
    pos = 11
    if has_s0:
        s0_ref = refs[pos]
        pos += 1
    o_ref, s_out_ref, s_ref = refs[pos:pos + 3]
    c = pl.program_id(1)
    nc = pl.num_programs(1)

    @pl.when(c == 0)
    def _():
        if has_s0:
            s_ref[...] = s0_ref[0]
        else:
            s_ref[...] = jnp.zeros_like(s_ref)

    cos = cos_ref[...]
    sin = sin_ref[...]

    def pad(t):
        if T == Cp:
            return t
        return jnp.concatenate([t, jnp.zeros((Cp - T, t.shape[1]), t.dtype)], axis=0)

    for h in range(RET_HEADS):
        sl = slice(h * RET_DK, (h + 1) * RET_DK)
        q = pad(q_ref[0, :, sl].astype(F32))
        k = pad(k_ref[0, :, sl].astype(F32))
        v = pad(v_ref[0, :, sl].astype(F32)).astype(BF16)
        q = q * cos + pltpu.roll(q, RET_DK // 2, 1) * sin
        k = (k * cos + pltpu.roll(k, RET_DK // 2, 1) * sin) * (RET_DK ** -0.5)
        qb = q.astype(BF16)
        att = _dot_nt(qb, k.astype(BF16)) * inner_ref[h]
        s_prev = s_ref[h]
        o = _dot(att.astype(BF16), v) + _dot(qb, s_prev.astype(BF16)) * qd_ref[h]
        s_ref[h] = s_prev * cd_ref[h] + _dot_tn((k * kd_ref[h]).astype(BF16), v)
        mu = jnp.mean(o, axis=-1, keepdims=True)
        d = o - mu
        var = jnp.mean(d * d, axis=-1, keepdims=True)
        y = d * lax.rsqrt(var + GN_EPS) * gn_ref[:, sl]
        gate = g_ref[0, :, sl].astype(F32)
        gate = gate * jax.nn.sigmoid(gate)
        o_ref[0, :, sl] = (gate * y[:T]).astype(o_ref.dtype)

    @pl.when(c == nc - 1)
    def _():
        s_out_ref[0] = s_ref[...]


def _retention_tables(C, Cp):
    H = RET_HEADS
    lg = jnp.log1p(-jnp.exp2(-5.0 - jnp.arange(H, dtype=F32)))
    i = jnp.arange(C, dtype=F32)
    diff = i[:, None] - i[None, :]
    inner = jnp.where((diff >= 0)[None], jnp.exp(jnp.maximum(diff, 0.0)[None] * lg[:, None, None]), 0.0)
    qd = jnp.exp((i[None, :] + 1.0) * lg[:, None])
    kd = jnp.exp((C - 1.0 - i)[None, :] * lg[:, None])
    cd = jnp.exp(C * lg)
    inner = jnp.pad(inner, ((0, 0), (0, Cp - C), (0, Cp - C)))
    qd = jnp.broadcast_to(jnp.pad(qd, ((0, 0), (0, Cp - C)))[:, :, None], (H, Cp, LANES))
    kd = jnp.broadcast_to(jnp.pad(kd, ((0, 0), (0, Cp - C)))[:, :, None], (H, Cp, LANES))
    cd = jnp.broadcast_to(cd[:, None, None], (H, 1, LANES))
    return inner, qd, kd, cd


def _rope_tables(pos, rows):
    half = RET_DK // 2
    inv = ROPE_BASE ** (-jnp.arange(half, dtype=F32) / half)
    ang = pos.astype(F32)[:, None] * inv[None, :]
    cos = jnp.cos(ang)
    sin = jnp.sin(ang)
    cos_full = jnp.concatenate([cos, cos], axis=-1)
    sin_signed = jnp.concatenate([-sin, sin], axis=-1)
    padn = rows - pos.shape[0]
    return jnp.pad(cos_full, ((0, padn), (0, 0))), jnp.pad(sin_signed, ((0, padn), (0, 0)))


def _retention(ret, pos, gn, s0, C, out_dtype):
    B, L, _ = ret.shape
    n = L // C
    Cp = max(C, LANES)
    inner, qd, kd, cd = _retention_tables(C, Cp)
    cos, sin = _rope_tables(pos, n * Cp)
    has_s0 = s0 is not None
    H = RET_HEADS

    def seg(k):
        return pl.BlockSpec((1, C, HEAD_COLS), lambda b, c, k=k: (b, c, k))

    const3 = lambda b, c: (0, 0, 0)
    in_specs = [seg(0), seg(1), seg(2), seg(3),
                pl.BlockSpec((Cp, LANES), lambda b, c: (c, 0)),
                pl.BlockSpec((Cp, LANES), lambda b, c: (c, 0)),
                pl.BlockSpec((H, Cp, Cp), const3),
                pl.BlockSpec((H, Cp, LANES), const3),
                pl.BlockSpec((H, Cp, LANES), const3),
                pl.BlockSpec((H, 1, LANES), const3),
                pl.BlockSpec((1, HEAD_COLS), lambda b, c: (0, 0))]
    args = [ret, ret, ret, ret, cos, sin, inner, qd, kd, cd, gn.reshape(1, HEAD_COLS)]
    if has_s0:
        in_specs.append(pl.BlockSpec((1, H, RET_DK, RET_DV), lambda b, c: (b, 0, 0, 0)))
        args.append(s0)
    return pl.pallas_call(
        functools.partial(_retention_kernel, T=C, Cp=Cp, has_s0=has_s0),
        grid=(B, n),
        in_specs=in_specs,
        out_specs=[pl.BlockSpec((1, C, HEAD_COLS), lambda b, c: (b, c, 0)),
                   pl.BlockSpec((1, H, RET_DK, RET_DV), lambda b, c: (b, 0, 0, 0))],
        out_shape=[jax.ShapeDtypeStruct((B, L, HEAD_COLS), out_dtype),
                   jax.ShapeDtypeStruct((B, H, RET_DK, RET_DV), F32)],
        scratch_shapes=[pltpu.VMEM((H, RET_DK, RET_DV), F32)],
        compiler_params=_params(("parallel", "arbitrary")),
        name="retention",
    )(*args)


def _split3(x):
    hi = x.astype(BF16).astype(F32)
    r = x - hi
    mid = r.astype(BF16).astype(F32)
    lo = (r - mid).astype(BF16).astype(F32)
    return hi, mid, lo


def _lane_scan(x, strict_suffix):
    j = lax.broadcasted_iota(jnp.int32, (LANES, LANES), 0)
    s = lax.broadcasted_iota(jnp.int32, (LANES, LANES), 1)
    tri = (j > s) if strict_suffix else (j <= s)
    rhs = jnp.concatenate([tri.astype(F32), jnp.ones((LANES, LANES), F32)], axis=1)
    hi, mid, lo = _split3(x)
    y = _dot(jnp.concatenate([hi, mid, lo], axis=0), rhs)
    y = y[0:8] + y[8:16] + y[16:24]
    return y[:, :LANES], y[:, LANES:]


def _cumsum_kernel(x_ref, o_ref, *, n_chunks):
    carry = jnp.zeros((8, LANES), F32)
    for c in range(n_chunks):
        sl = slice(c * LANES, (c + 1) * LANES)
        scan, tot = _lane_scan(x_ref[0, :, sl], strict_suffix=False)
        o_ref[0, :, sl] = scan + carry
        carry = carry + tot


def _cumsum_lanes(x):
    B, H, N = x.shape
    return pl.pallas_call(
        functools.partial(_cumsum_kernel, n_chunks=N // LANES),
        grid=(B,),
        in_specs=[pl.BlockSpec((1, H, N), lambda b: (b, 0, 0))],
        out_specs=pl.BlockSpec((1, H, N), lambda b: (b, 0, 0)),
        out_shape=jax.ShapeDtypeStruct((B, H, N), F32),
        compiler_params=_params(("parallel",)),
        name="logf_cumsum",
    )(x)


def _fox_prompt_kernel(q_ref, k_ref, v_ref, cq_ref, ck_ref, o_ref, m_ref, l_ref, acc_ref, *, tq):
    qi = pl.program_id(1)
    ki = pl.program_id(2)
    scale = FOX_HD ** -0.5

    @pl.when(ki == 0)
    def _():
        m_ref[...] = jnp.full_like(m_ref, -jnp.inf)
        l_ref[...] = jnp.zeros_like(l_ref)
        acc_ref[...] = jnp.zeros_like(acc_ref)

    def step(diagonal):
        if diagonal:
            row = lax.broadcasted_iota(jnp.int32, (tq, tq), 0)
            col = lax.broadcasted_iota(jnp.int32, (tq, tq), 1)
            causal = row >= col
        for h in range(FOX_HEADS):
            sl = slice(h * FOX_HD, (h + 1) * FOX_HD)
            k = (k_ref[0, :, sl] * scale).astype(BF16)
            v = v_ref[0, :, sl].astype(BF16)
            s = _dot_nt(q_ref[0, :, sl], k)
            s = s + (cq_ref[0, :, h:h + 1] - ck_ref[0, h:h + 1, :])
            if diagonal:
                s = jnp.where(causal, s, -jnp.inf)
            m_prev = m_ref[h]
            m_new = jnp.maximum(m_prev, jnp.max(s, axis=1, keepdims=True))
            corr = jnp.exp(m_prev - m_new)
            p = jnp.exp(s - m_new)
            l_ref[h] = l_ref[h] * corr + jnp.sum(p, axis=1, keepdims=True)
            acc_ref[:, sl] = acc_ref[:, sl] * corr + _dot(p.astype(BF16), v)
            m_ref[h] = m_new

    @pl.when(ki < qi)
    def _():
        step(False)

    @pl.when(ki == qi)
    def _():
        step(True)
        for h in range(FOX_HEADS):
            sl = slice(h * FOX_HD, (h + 1) * FOX_HD)
            o_ref[0, :, sl] = (acc_ref[:, sl] / l_ref[h]).astype(o_ref.dtype)


def _fox_prompt(fq, fk, fv, ct_col, ct_row, tq):
    B, L, _ = fq.shape
    n = L // tq
    kv_map = lambda b, qi, ki: (b, jnp.minimum(ki, qi), 0)
    return pl.pallas_call(
        functools.partial(_fox_prompt_kernel, tq=tq),
        grid=(B, n, n),
        in_specs=[pl.BlockSpec((1, tq, HEAD_COLS), lambda b, qi, ki: (b, qi, 0)),
                  pl.BlockSpec((1, tq, HEAD_COLS), kv_map),
                  pl.BlockSpec((1, tq, HEAD_COLS), kv_map),
                  pl.BlockSpec((1, tq, FOX_HEADS), lambda b, qi, ki: (b, qi, 0)),
                  pl.BlockSpec((1, FOX_HEADS, tq), lambda b, qi, ki: (b, 0, jnp.minimum(ki, qi)))],
        out_specs=pl.BlockSpec((1, tq, HEAD_COLS), lambda b, qi, ki: (b, qi, 0)),
        out_shape=jax.ShapeDtypeStruct((B, L, HEAD_COLS), BF16),
        scratch_shapes=[pltpu.VMEM((FOX_HEADS, tq, 1), F32),
                        pltpu.VMEM((FOX_HEADS, tq, 1), F32),
                        pltpu.VMEM((tq, HEAD_COLS), F32)],
        compiler_params=_params(("parallel", "parallel", "arbitrary")),
        name="fox_prompt",
    )(fq, fk, fv, ct_col, ct_row)


def _mem_attend_kernel(q_ref, k_ref, v_ref, o_ref):
    scale = MEM_HD ** -0.5
    for h in range(MEM_HEADS):
        sl = slice(h * MEM_HD, (h + 1) * MEM_HD)
        q = q_ref[0, :, sl].astype(BF16)
        k = (k_ref[0, :, sl] * scale).astype(BF16)
        v = v_ref[0, :, sl].astype(BF16)
        s = _dot_nt(q, k)
        p = jnp.exp(s - jnp.max(s, axis=1, keepdims=True))
        o = _dot(p.astype(BF16), v) / jnp.sum(p, axis=1, keepdims=True)
        o_ref[0, :, sl] = o.astype(o_ref.dtype)


def _mem_attend(mq, mk, mv, tq, out_dtype):
    B, L, _ = mq.shape
    M = mk.shape[1]
    return pl.pallas_call(
        _mem_attend_kernel,
        grid=(B, L // tq),
        in_specs=[pl.BlockSpec((1, tq, HEAD_COLS), lambda b, i: (b, i, 0)),
                  pl.BlockSpec((1, M, HEAD_COLS), lambda b, i: (b, 0, 0)),
                  pl.BlockSpec((1, M, HEAD_COLS), lambda b, i: (b, 0, 0))],
        out_specs=pl.BlockSpec((1, tq, HEAD_COLS), lambda b, i: (b, i, 0)),
        out_shape=jax.ShapeDtypeStruct((B, L, HEAD_COLS), out_dtype),
        compiler_params=_params(("parallel", "parallel")),
        name="mem_attend",
    )(mq, mk, mv)


def _merge_kernel(x_ref, oret_ref, ofox_ref, omem_ref, gate_ref, wr_ref, wf_ref, wm_ref, wo_ref, o_ref, *, D):
    def branch(k, o_b, w_ref):
        g = jax.nn.sigmoid(gate_ref[:, k * D:(k + 1) * D].astype(F32))
        return g * _dot(o_b[...].astype(BF16), w_ref[...])

    merged = branch(0, oret_ref, wr_ref) + branch(1, ofox_ref, wf_ref) + branch(2, omem_ref, wm_ref)
    o_ref[...] = x_ref[...] + _dot(merged.astype(BF16), wo_ref[...])


def _merge(x2d, o_ret, o_fox, o_mem, gates, w_ret, w_fox, w_mem, w_o, tm):
    T, D = x2d.shape
    row = lambda i: (i, 0)
    const = lambda i: (0, 0)
    resident = functools.partial(pl.BlockSpec, index_map=const, pipeline_mode=pl.Buffered(1))
    return pl.pallas_call(
        functools.partial(_merge_kernel, D=D),
        grid=(T // tm,),
        in_specs=[pl.BlockSpec((tm, D), row),
                  pl.BlockSpec((tm, HEAD_COLS), row),
                  pl.BlockSpec((tm, HEAD_COLS), row),
                  pl.BlockSpec((tm, HEAD_COLS), row),
                  pl.BlockSpec((tm, N_BRANCH * D), row),
                  resident((HEAD_COLS, D)),
                  resident((HEAD_COLS, D)),
                  resident((HEAD_COLS, D)),
                  resident((D, D))],
        out_specs=pl.BlockSpec((tm, D), row),
        out_shape=jax.ShapeDtypeStruct((T, D), F32),
        compiler_params=_params(("parallel",)),
        name="merge",
    )(x2d, o_ret, o_fox, o_mem, gates, w_ret, w_fox, w_mem, w_o)


def _mlp_kernel(x_ref, g_ref, wu_ref, wd_ref, gf_ref, o_ref, y_ref, hn_ref):
    f = pl.program_id(1)
    nf = pl.num_programs(1)

    @pl.when(f == 0)
    def _():
        x = x_ref[...]
        hn_ref[...] = _rms_normed(x, g_ref[...]).astype(BF16)
        o_ref[...] = x

    u = jnp.maximum(_dot(hn_ref[...], wu_ref[...]), 0.0)
    o_ref[...] += _dot((u * u).astype(BF16), wd_ref[...])

    @pl.when(f == nf - 1)
    def _():
        y_ref[...] = _rms_normed(o_ref[...], gf_ref[...])


def _mlp(x2d, g, w_up, w_down, g_final, tm, tf):
    T, D = x2d.shape
    Fd = w_up.shape[1]
    return pl.pallas_call(
        _mlp_kernel,
        grid=(T // tm, Fd // tf),
        in_specs=[pl.BlockSpec((tm, D), lambda i, f: (i, 0), pipeline_mode=pl.Buffered(1)),
                  pl.BlockSpec((1, D), lambda i, f: (0, 0)),
                  pl.BlockSpec((D, tf), lambda i, f: (0, f)),
                  pl.BlockSpec((tf, D), lambda i, f: (f, 0)),
                  pl.BlockSpec((1, D), lambda i, f: (0, 0))],
        out_specs=[pl.BlockSpec((tm, D), lambda i, f: (i, 0)),
                   pl.BlockSpec((tm, D), lambda i, f: (i, 0))],
        out_shape=[jax.ShapeDtypeStruct((T, D), F32), jax.ShapeDtypeStruct((T, D), F32)],
        scratch_shapes=[pltpu.VMEM((tm, D), BF16)],
        compiler_params=_params(("parallel", "arbitrary")),
        name="mlp",
    )(x2d, g.reshape(1, D), w_up, w_down, g_final.reshape(1, D))


def _fox_sample_kernel(pt_ref, q_ref, kn_ref, vn_ref, ptn_ref, *refs, G, T, NG):
    k_refs = refs[:G]
    v_refs = refs[G:2 * G]
    lf_refs = refs[2 * G:3 * G]
    o_ref, qbd_ref, ptcol_ref, m_ref, l_ref, acc_ref, carry_ref = refs[3 * G:]
    del pt_ref
    g = pl.program_id(1)
    H = FOX_HEADS
    R = H * T

    def expand_rows(x):
        return jnp.broadcast_to(x[:, None, :], (H, T, x.shape[1])).reshape(R, x.shape[1])

    def update(s, v_list):
        m_prev = m_ref[...]
        m_new = jnp.maximum(m_prev, jnp.max(s, axis=1, keepdims=True))
        corr = jnp.exp(m_prev - m_new)
        p = jnp.exp(s - m_new)
        l_ref[...] = l_ref[...] * corr + jnp.sum(p, axis=1, keepdims=True)
        pv = _dot(p[:, :LANES].astype(BF16), v_list[0])
        for i in range(1, len(v_list)):
            pv = pv + _dot(p[:, i * LANES:(i + 1) * LANES].astype(BF16), v_list[i])
        acc_ref[...] = acc_ref[...] * corr + pv
        m_ref[...] = m_new

    @pl.when(g == 0)
    def _():
        q = q_ref[0] * (FOX_HD ** -0.5)
        qt = jnp.concatenate([q] * H, axis=0)
        rh = lax.broadcasted_iota(jnp.int32, (R, HEAD_COLS), 0) // T
        ch = lax.broadcasted_iota(jnp.int32, (R, HEAD_COLS), 1) // FOX_HD
        qbd = jnp.where(rh == ch, qt, 0.0).astype(BF16)
        qbd_ref[...] = qbd
        ptexp = expand_rows(ptn_ref[0])
        lane = lax.broadcasted_iota(jnp.int32, (R, LANES), 1)
        rq = lax.broadcasted_iota(jnp.int32, (R, LANES), 0) % T
        ptcol = jnp.sum(jnp.where(lane == rq, ptexp, 0.0), axis=1, keepdims=True)
        ptcol_ref[...] = ptcol
        zpad = jnp.zeros((LANES - T, HEAD_COLS), F32)
        kn = jnp.concatenate([kn_ref[0], zpad], axis=0).astype(BF16)
        vn = jnp.concatenate([vn_ref[0], zpad], axis=0).astype(BF16)
        s = _dot_nt(qbd, kn) + (ptcol - ptexp)
        s = jnp.where(lane <= rq, s, -jnp.inf)
        m_ref[...] = jnp.full_like(m_ref, -jnp.inf)
        l_ref[...] = jnp.zeros_like(l_ref)
        acc_ref[...] = jnp.zeros_like(acc_ref)
        carry_ref[...] = jnp.zeros_like(carry_ref)
        update(s, [vn])

    qbd = qbd_ref[...]
    carry = carry_ref[...]
    s_list = []
    for i in range(G):
        suffix, tot = _lane_scan(lf_refs[i][0, 0], strict_suffix=True)
        bias = expand_rows(suffix + carry)
        carry = carry + tot
        s_list.append(_dot_nt(qbd, k_refs[i][0, 0].astype(BF16)) + bias)
    carry_ref[...] = carry
    s = jnp.concatenate(s_list, axis=1) + ptcol_ref[...]
    update(s, [v_refs[i][0, 0].astype(BF16) for i in range(G)])

    @pl.when(g == NG - 1)
    def _():
        for h in range(H):
            rows = slice(h * T, (h + 1) * T)
            cols = slice(h * FOX_HD, (h + 1) * FOX_HD)
            o_ref[0, :, cols] = acc_ref[rows, cols] / l_ref[rows, :]


def _fox_sample(fq, fk, fv, pt_new, cache_k, cache_v, cache_lf_t, page_table, layer, G):
    B, T, _ = fq.shape
    n_pages = page_table.shape[1]
    ps = cache_k.shape[2]
    assert ps == LANES and n_pages % G == 0
    NG = n_pages // G
    R = FOX_HEADS * T
    tok = pl.BlockSpec((1, T, HEAD_COLS), lambda b, g, pt: (b, 0, 0))

    def page_spec(i, shape):
        return pl.BlockSpec((1, 1) + shape,
                            lambda b, g, pt, i=i: (layer, pt[b, n_pages - 1 - (g * G + i)], 0, 0))

    in_specs = ([tok, tok, tok, pl.BlockSpec((1, FOX_HEADS, LANES), lambda b, g, pt: (b, 0, 0))]
                + [page_spec(i, (ps, HEAD_COLS)) for i in range(G)]
                + [page_spec(i, (ps, HEAD_COLS)) for i in range(G)]
                + [page_spec(i, (FOX_HEADS, ps)) for i in range(G)])
    grid_spec = pltpu.PrefetchScalarGridSpec(
        num_scalar_prefetch=1,
        grid=(B, NG),
        in_specs=in_specs,
        out_specs=pl.BlockSpec((1, T, HEAD_COLS), lambda b, g, pt: (b, 0, 0)),
        scratch_shapes=[pltpu.VMEM((R, HEAD_COLS), BF16),
                        pltpu.VMEM((R, 1), F32),
                        pltpu.VMEM((R, 1), F32),
                        pltpu.VMEM((R, 1), F32),
                        pltpu.VMEM((R, HEAD_COLS), F32),
                        pltpu.VMEM((FOX_HEADS, LANES), F32)])
    return pl.pallas_call(
        functools.partial(_fox_sample_kernel, G=G, T=T, NG=NG),
        grid_spec=grid_spec,
        out_shape=jax.ShapeDtypeStruct((B, T, HEAD_COLS), F32),
        compiler_params=_params(("parallel", "arbitrary")),
        name="fox_sample",
    )(page_table, fq, fk, fv, pt_new, *([cache_k] * G), *([cache_v] * G), *([cache_lf_t] * G))


def _pick(n, prefs):
    for p in prefs:
        if n % p == 0:
            return p
    return n


def _in_proj(x2d, g, w_in_l, b_ff_l, inter_dtype, tm, tn):
    D = x2d.shape[1]
    n_head = 7 * HEAD_COLS
    w_main = jnp.concatenate([w_in_l[:, :n_head], w_in_l[:, n_head + FOX_HEADS:]], axis=1).astype(BF16)
    w_ff = jnp.pad(w_in_l[:, n_head:n_head + FOX_HEADS], ((0, 0), (0, LANES - FOX_HEADS))).astype(BF16)
    b_ff = jnp.pad(b_ff_l.astype(F32), (0, LANES - FOX_HEADS)).reshape(1, LANES)
    hb = HEAD_COLS // tn
    segs = [(4 * hb, inter_dtype), (hb, inter_dtype), (hb, F32), (hb, F32), (hb, inter_dtype),
            (N_BRANCH * D // tn, inter_dtype)]
    ret, fq, fk, fv, mq, gates, logf_pad = _norm_proj(x2d, g, w_main, segs, tm, tn, w_ff, b_ff)
    return ret, fq, fk, fv, mq, gates, logf_pad[:, :FOX_HEADS]


def kernel(x_prompt, x_sample, mem_prompt, state_ret, cache_fox_k, cache_fox_v, cache_fox_logf,
           cache_mem_k, cache_mem_v, page_table, norm_mix, w_in, b_fox_f, gn_ret, norm_mem, w_mem_kv,
           w_br_ret, w_br_fox, w_br_mem, w_o, norm_mlp, w_up, w_down, norm_final):
    B, L, D = x_prompt.shape
    Bd, T, _ = x_sample.shape
    depth = w_in.shape[0]
    n_pool, ps = cache_fox_k.shape[1], cache_fox_k.shape[2]
    n_pages = page_table.shape[1]
    M = mem_prompt.shape[1]
    past_len = n_pages * ps
    pos_p = jnp.arange(L, dtype=jnp.int32)
    pos_s = past_len + jnp.arange(T, dtype=jnp.int32)

    tn = _pick(D, (1024, 256))
    tm_p = _pick(B * L, (512, 256))
    C_ret = _pick(L, (256, 128))
    tq_fox = _pick(L, (512, 256, 128))
    tq_mem = _pick(L, (512, 256, 128))
    tm_merge = _pick(B * L, (256,))
    tm_mlp = _pick(B * L, (1024, 512, 256))
    tf_mlp = _pick(w_up.shape[2], (512,))
    G = _pick(n_pages, (8, 4, 2))

    cache_k = cache_fox_k.reshape(depth, n_pool, ps, HEAD_COLS)
    cache_v = cache_fox_v.reshape(depth, n_pool, ps, HEAD_COLS)
    cache_lf_t = jnp.swapaxes(cache_fox_logf, 2, 3)

    xp = x_prompt.reshape(B * L, D)
    xs = x_sample.reshape(Bd * T, D)
    outs = {k: [] for k in ("sp", "ss", "kp", "vp", "fp", "ks", "vs", "fs", "mk", "mv")}
    yp = ys = None
    for l in range(depth):
        w_ret = w_br_ret[l].astype(BF16)
        w_fox = w_br_fox[l].astype(BF16)
        w_mem = w_br_mem[l].astype(BF16)
        w_o_l = w_o[l].astype(BF16)
        w_up_l = w_up[l].astype(BF16)
        w_down_l = w_down[l].astype(BF16)
        g_fin = norm_final if l == depth - 1 else jnp.ones_like(norm_final)

        ret, fq, fk, fv, mq, gates, logf = _in_proj(xp, norm_mix[l], w_in[l], b_fox_f[l], BF16, tm_p, tn)
        o_ret, s_new = _retention(ret.reshape(B, L, 4 * HEAD_COLS), pos_p, gn_ret[l], None, C_ret, BF16)
        logf3 = logf.reshape(B, L, FOX_HEADS)
        ct_row = _cumsum_lanes(jnp.swapaxes(logf3, 1, 2))
        o_fox = _fox_prompt(fq.reshape(B, L, HEAD_COLS), fk.reshape(B, L, HEAD_COLS),
                            fv.reshape(B, L, HEAD_COLS), jnp.swapaxes(ct_row, 1, 2), ct_row, tq_fox)
        mk, mv = _norm_proj(mem_prompt.reshape(B * M, D), norm_mem[l], w_mem_kv[l].astype(BF16),
                            [(HEAD_COLS // tn, F32), (HEAD_COLS // tn, F32)], _pick(B * M, (512, 256)), tn)
        o_mem = _mem_attend(mq.reshape(B, L, HEAD_COLS), mk.reshape(B, M, HEAD_COLS),
                            mv.reshape(B, M, HEAD_COLS), tq_mem, BF16)
        xp = _merge(xp, o_ret.reshape(B * L, HEAD_COLS), o_fox.reshape(B * L, HEAD_COLS),
                    o_mem.reshape(B * L, HEAD_COLS), gates, w_ret, w_fox, w_mem, w_o_l, tm_merge)
        xp, yp = _mlp(xp, norm_mlp[l], w_up_l, w_down_l, g_fin, tm_mlp, tf_mlp)
        outs["sp"].append(s_new)
        outs["kp"].append(fk.reshape(B, L, FOX_HEADS, FOX_HD))
        outs["vp"].append(fv.reshape(B, L, FOX_HEADS, FOX_HD))
        outs["fp"].append(logf3)
        outs["mk"].append(mk.reshape(B, M, MEM_HEADS, MEM_HD))
        outs["mv"].append(mv.reshape(B, M, MEM_HEADS, MEM_HD))

        ret, fq, fk, fv, mq, gates, logf = _in_proj(xs, norm_mix[l], w_in[l], b_fox_f[l], F32, Bd * T, tn)
        o_ret, s_new = _retention(ret.reshape(Bd, T, 4 * HEAD_COLS), pos_s, gn_ret[l], state_ret[l], T, F32)
        logf3 = logf.reshape(Bd, T, FOX_HEADS)
        pt_new = _cumsum_lanes(jnp.pad(jnp.swapaxes(logf3, 1, 2), ((0, 0), (0, 0), (0, LANES - T))))
        o_fox = _fox_sample(fq.reshape(Bd, T, HEAD_COLS), fk.reshape(Bd, T, HEAD_COLS),
                            fv.reshape(Bd, T, HEAD_COLS), pt_new, cache_k, cache_v, cache_lf_t,
                            page_table, l, G)
        o_mem = _mem_attend(mq.reshape(Bd, T, HEAD_COLS), cache_mem_k[l].reshape(Bd, M, HEAD_COLS),
                            cache_mem_v[l].reshape(Bd, M, HEAD_COLS), T, F32)
        xs = _merge(xs, o_ret.reshape(Bd * T, HEAD_COLS), o_fox.reshape(Bd * T, HEAD_COLS),
                    o_mem.reshape(Bd * T, HEAD_COLS), gates, w_ret, w_fox, w_mem, w_o_l, Bd * T)
        xs, ys = _mlp(xs, norm_mlp[l], w_up_l, w_down_l, g_fin, Bd * T, tf_mlp)
        outs["ss"].append(s_new)
        outs["ks"].append(fk.reshape(Bd, T, FOX_HEADS, FOX_HD))
        outs["vs"].append(fv.reshape(Bd, T, FOX_HEADS, FOX_HD))
        outs["fs"].append(logf3)

    st = lambda k: jnp.stack(outs[k])
    return (yp.reshape(B, L, D), ys.reshape(Bd, T, D), st("sp"), st("ss"), st("kp"), st("vp"), st("fp"),
            st("ks"), st("vs"), st("fs"), st("mk"), st("mv"))
```

```python
import functools
import math

import jax
import jax.numpy as jnp
from jax import lax
from jax.experimental import pallas as pl
from jax.experimental.pallas import tpu as pltpu

F32 = jnp.float32
BF16 = jnp.bfloat16

RET_HEADS = 8
RET_DK = 128
RET_DV = 128
FOX_HEADS = 8
FOX_HD = 128
MEM_HEADS = 4
MEM_HD = 256
N_BRANCH = 3
ROPE_BASE = 10000.0
RMS_EPS = 1e-6
GN_EPS = 1e-5

HEAD_COLS = 1024
LANES = 128
SUBLANES = 8
VMEM_LIMIT = 56 * 1024 * 1024
LOG2E = math.log2(math.e)

COL_FQ = 4
COL_MQ = 5
COL_GATES = 6


def _dot(a, b):
    return jnp.dot(a, b, preferred_element_type=F32)


def _dot_nt(a, b):
    return lax.dot_general(a, b, (((1,), (1,)), ((), ())), preferred_element_type=F32)


def _dot_tn(a, b):
    return lax.dot_general(a, b, (((0,), (0,)), ((), ())), preferred_element_type=F32)


def _params(sem):
    return pltpu.CompilerParams(dimension_semantics=sem, vmem_limit_bytes=VMEM_LIMIT)


def _rms_normed(x, g):
    ms = jnp.mean(x * x, axis=-1, keepdims=True)
    return x * lax.rsqrt(ms + RMS_EPS) * g


def _log_sigmoid(x):
    return jnp.minimum(x, 0.0) - jnp.log1p(jnp.exp(-jnp.abs(x)))


def _norm_proj_kernel(x_ref, g_ref, w_ref, o_ref, hn_ref):
    @pl.when(pl.program_id(1) == 0)
    def _():
        hn_ref[...] = _rms_normed(x_ref[...], g_ref[...]).astype(BF16)

    o_ref[...] = _dot(hn_ref[...], w_ref[...]).astype(o_ref.dtype)


def _norm_proj(x2d, g, w, tm, tn, out_dtype):
    T, D = x2d.shape
    N = w.shape[1]
    return pl.pallas_call(
        _norm_proj_kernel,
        grid=(T // tm, N // tn),
        in_specs=[pl.BlockSpec((tm, D), lambda i, j: (i, 0)),
                  pl.BlockSpec((1, D), lambda i, j: (0, 0)),
                  pl.BlockSpec((D, tn), lambda i, j: (0, j))],
        out_specs=pl.BlockSpec((tm, tn), lambda i, j: (i, j)),
        out_shape=jax.ShapeDtypeStruct((T, N), out_dtype),
        scratch_shapes=[pltpu.VMEM((tm, D), BF16)],
        compiler_params=_params(("parallel", "arbitrary")),
        name="norm_proj",
    )(x2d, g.reshape(1, D), w)


def _norm_kv_kernel(*refs, has_ff):
    if has_ff:
        x_ref, g_ref, w_ref, wff_ref, bff_ref, k_ref, v_ref, logf_ref = refs
    else:
        x_ref, g_ref, w_ref, k_ref, v_ref = refs
    hn = _rms_normed(x_ref[...], g_ref[...]).astype(BF16)
    kv = _dot(hn, w_ref[...])
    k_ref[...] = kv[:, :HEAD_COLS]
    v_ref[...] = kv[:, HEAD_COLS:]
    if has_ff:
        logf_ref[...] = _log_sigmoid(_dot(hn, wff_ref[...]) + bff_ref[...])


def _norm_kv(x2d, g, w, tm, w_ff=None, b_ff=None):
    T, D = x2d.shape
    has_ff = w_ff is not None
    row = lambda i: (i, 0)
    const = lambda i: (0, 0)
    in_specs = [pl.BlockSpec((tm, D), row), pl.BlockSpec((1, D), const),
                pl.BlockSpec((D, 2 * HEAD_COLS), const, pipeline_mode=pl.Buffered(1))]
    args = [x2d, g.reshape(1, D), w]
    out_specs = [pl.BlockSpec((tm, HEAD_COLS), row), pl.BlockSpec((tm, HEAD_COLS), row)]
    out_shape = [jax.ShapeDtypeStruct((T, HEAD_COLS), F32), jax.ShapeDtypeStruct((T, HEAD_COLS), F32)]
    if has_ff:
        in_specs += [pl.BlockSpec((D, LANES), const), pl.BlockSpec((1, LANES), const)]
        args += [w_ff, b_ff]
        out_specs.append(pl.BlockSpec((tm, LANES), row))
        out_shape.append(jax.ShapeDtypeStruct((T, LANES), F32))
    return pl.pallas_call(
        functools.partial(_norm_kv_kernel, has_ff=has_ff),
        grid=(T // tm,),
        in_specs=in_specs,
        out_specs=out_specs,
        out_shape=out_shape,
        compiler_params=_params(("parallel",)),
        name="norm_kv",
    )(*args)


def _retention_kernel(*refs, T, Cp, has_s0):
    q_ref, k_ref, v_ref, g_ref, cos_ref, sin_ref, inner_ref, qd_ref, kd_ref, cd_ref, gn_ref = refs[:11]
    pos = 11
    if has_s0:
        s0_ref = refs[pos]
        pos += 1
    o_ref, s_out_ref, s_ref = refs[pos:pos + 3]
    c = pl.program_id(1)
    nc = pl.num_programs(1)

    @pl.when(c == 0)
    def _():
        if has_s0:
            s_ref[...] = s0_ref[0]
        else:
            s_ref[...] = jnp.zeros_like(s_ref)

    cos = cos_ref[...]
    sin = sin_ref[...]

    def pad(t):
        if T == Cp:
            return t
        return jnp.concatenate([t, jnp.zeros((Cp - T, t.shape[1]), t.dtype)], axis=0)

    for h in range(RET_HEADS):
        sl = slice(h * RET_DK, (h + 1) * RET_DK)
        q = pad(q_ref[0, :, sl].astype(F32))
        k = pad(k_ref[0, :, sl].astype(F32))
        v = pad(v_ref[0, :, sl].astype(F32)).astype(BF16)
        q = q * cos + pltpu.roll(q, RET_DK // 2, 1) * sin
        k = (k * cos + pltpu.roll(k, RET_DK // 2, 1) * sin) * (RET_DK ** -0.5)
        qb = q.astype(BF16)
        att = _dot_nt(qb, k.astype(BF16)) * inner_ref[h]
        s_prev = s_ref[h]
        o = _dot(att.astype(BF16), v) + _dot(qb, s_prev.astype(BF16)) * qd_ref[h]
        s_ref[h] = s_prev * cd_ref[h] + _dot_tn((k * kd_ref[h]).astype(BF16), v)
        mu = jnp.mean(o, axis=-1, keepdims=True)
        d = o - mu
        var = jnp.mean(d * d, axis=-1, keepdims=True)
        y = d * lax.rsqrt(var + GN_EPS) * gn_ref[:, sl]
        gate = g_ref[0, :, sl].astype(F32)
        gate = gate * jax.nn.sigmoid(gate)
        o_ref[0, :, sl] = (gate * y[:T]).astype(o_ref.dtype)

    @pl.when(c == nc - 1)
    def _():
        s_out_ref[0] = s_ref[...]


def _retention_tables(C, Cp):
    H = RET_HEADS
    lg = jnp.log1p(-jnp.exp2(-5.0 - jnp.arange(H, dtype=F32)))
    i = jnp.arange(C, dtype=F32)
    diff = i[:, None] - i[None, :]
    inner = jnp.where((diff >= 0)[None], jnp.exp(jnp.maximum(diff, 0.0)[None] * lg[:, None, None]), 0.0)
    qd = jnp.exp((i[None, :] + 1.0) * lg[:, None])
    kd = jnp.exp((C - 1.0 - i)[None, :] * lg[:, None])
    cd = jnp.exp(C * lg)
    inner = jnp.pad(inner, ((0, 0), (0, Cp - C), (0, Cp - C)))
    qd = jnp.broadcast_to(jnp.pad(qd, ((0, 0), (0, Cp - C)))[:, :, None], (H, Cp, LANES))
    kd = jnp.broadcast_to(jnp.pad(kd, ((0, 0), (0, Cp - C)))[:, :, None], (H, Cp, LANES))
    cd = jnp.broadcast_to(cd[:, None, None], (H, 1, LANES))
    return inner, qd, kd, cd


def _rope_tables(pos, rows):
    half = RET_DK // 2
    inv = ROPE_BASE ** (-jnp.arange(half, dtype=F32) / half)
    ang = pos.astype(F32)[:, None] * inv[None, :]
    cos = jnp.cos(ang)
    sin = jnp.sin(ang)
    cos_full = jnp.concatenate([cos, cos], axis=-1)
    sin_signed = jnp.concatenate([-sin, sin], axis=-1)
    padn = rows - pos.shape[0]
    return jnp.pad(cos_full, ((0, padn), (0, 0))), jnp.pad(sin_signed, ((0, padn), (0, 0)))


def _retention(z, pos, gn, s0, C, out_dtype):
    B, L, _ = z.shape
    n = L // C
    Cp = max(C, LANES)
    inner, qd, kd, cd = _retention_tables(C, Cp)
    cos, sin = _rope_tables(pos, n * Cp)
    has_s0 = s0 is not None
    H = RET_HEADS

    def seg(k):
        return pl.BlockSpec((1, C, HEAD_COLS), lambda b, c, k=k: (b, c, k))

    const3 = lambda b, c: (0, 0, 0)
    in_specs = [seg(0), seg(1), seg(2), seg(3),
                pl.BlockSpec((Cp, LANES), lambda b, c: (c, 0)),
                pl.BlockSpec((Cp, LANES), lambda b, c: (c, 0)),
                pl.BlockSpec((H, Cp, Cp), const3),
                pl.BlockSpec((H, Cp, LANES), const3),
                pl.BlockSpec((H, Cp, LANES), const3),
                pl.BlockSpec((H, 1, LANES), const3),
                pl.BlockSpec((1, HEAD_COLS), lambda b, c: (0, 0))]
    args = [z, z, z, z, cos, sin, inner, qd, kd, cd, gn.reshape(1, HEAD_COLS)]
    if has_s0:
        in_specs.append(pl.BlockSpec((1, H, RET_DK, RET_DV), lambda b, c: (b, 0, 0, 0)))
        args.append(s0)
    return pl.pallas_call(
        functools.partial(_retention_kernel, T=C, Cp=Cp, has_s0=has_s0),
        grid=(B, n),
        in_specs=in_specs,
        out_specs=[pl.BlockSpec((1, C, HEAD_COLS), lambda b, c: (b, c, 0)),
                   pl.BlockSpec((1, H, RET_DK, RET_DV), lambda b, c: (b, 0, 0, 0))],
        out_shape=[jax.ShapeDtypeStruct((B, L, HEAD_COLS), out_dtype),
                   jax.ShapeDtypeStruct((B, H, RET_DK, RET_DV), F32)],
        scratch_shapes=[pltpu.VMEM((H, RET_DK, RET_DV), F32)],
        compiler_params=_params(("parallel", "arbitrary")),
        name="retention",
    )(*args)


def _split3(x):
    hi = x.astype(BF16).astype(F32)
    r = x - hi
    mid = r.astype(BF16).astype(F32)
    lo = (r - mid).astype(BF16).astype(F32)
    return hi, mid, lo


def _lane_scan(x, stride, strict_suffix):
    n = x.shape[0]
    j = lax.broadcasted_iota(jnp.int32, (LANES, LANES), 0)
    s = lax.broadcasted_iota(jnp.int32, (LANES, LANES), 1)
    same = ((j - s) & (stride - 1)) == 0
    tri = jnp.logical_and(same, (j > s) if strict_suffix else (j <= s))
    rhs = jnp.concatenate([jnp.where(tri, 1.0, 0.0), jnp.where(same, 1.0, 0.0)], axis=1).astype(F32)
    hi, mid, lo = _split3(x)
    y = _dot(jnp.concatenate([hi, mid, lo], axis=0), rhs)
    y = y[0:n] + y[n:2 * n] + y[2 * n:3 * n]
    return y[:, :LANES], y[:, LANES:]


def _cumsum_kernel(x_ref, o_ref, *, n_chunks):
    carry = jnp.zeros((SUBLANES, LANES), F32)
    for c in range(n_chunks):
        sl = slice(c * LANES, (c + 1) * LANES)
        scan, tot = _lane_scan(x_ref[0, :, sl], 1, strict_suffix=False)
        o_ref[0, :, sl] = scan + carry
        carry = carry + tot


def _cumsum_lanes(x):
    B, H, N = x.shape
    return pl.pallas_call(
        functools.partial(_cumsum_kernel, n_chunks=N // LANES),
        grid=(B,),
        in_specs=[pl.BlockSpec((1, H, N), lambda b: (b, 0, 0))],
        out_specs=pl.BlockSpec((1, H, N), lambda b: (b, 0, 0)),
        out_shape=jax.ShapeDtypeStruct((B, H, N), F32),
        compiler_params=_params(("parallel",)),
        name="logf_cumsum",
    )(x)


def _fox_prompt_kernel(q_ref, k_ref, v_ref, cq_ref, ck_ref, o_ref, m_ref, l_ref, acc_ref, *, tq):
    qi = pl.program_id(1)
    ki = pl.program_id(2)
    kscale = (FOX_HD ** -0.5) * LOG2E

    @pl.when(ki == 0)
    def _():
        m_ref[...] = jnp.full_like(m_ref, -jnp.inf)
        l_ref[...] = jnp.zeros_like(l_ref)
        acc_ref[...] = jnp.zeros_like(acc_ref)

    def scores(h):
        sl = slice(h * FOX_HD, (h + 1) * FOX_HD)
        k = (k_ref[0, :, sl] * kscale).astype(BF16)
        return _dot_nt(q_ref[0, :, sl], k)

    def step(diagonal):
        if diagonal:
            row = lax.broadcasted_iota(jnp.int32, (tq, tq), 0)
            col = lax.broadcasted_iota(jnp.int32, (tq, tq), 1)
            causal = row >= col
        ck = ck_ref[0] * LOG2E
        s_next = scores(0)
        for h in range(FOX_HEADS):
            sl = slice(h * FOX_HD, (h + 1) * FOX_HD)
            s = s_next
            if h + 1 < FOX_HEADS:
                s_next = scores(h + 1)
            t = s - ck[h:h + 1, :]
            if diagonal:
                t = jnp.where(causal, t, -jnp.inf)
            cq = cq_ref[0, :, h:h + 1] * LOG2E
            m_prev = m_ref[h]
            m_new = jnp.maximum(m_prev, jnp.max(t, axis=1, keepdims=True) + cq)
            corr = jnp.exp2(m_prev - m_new)
            p = jnp.exp2(t - (m_new - cq))
            l_ref[h] = l_ref[h] * corr + jnp.sum(p, axis=1, keepdims=True)
            v = v_ref[0, :, sl].astype(BF16)
            acc_ref[:, sl] = acc_ref[:, sl] * corr + _dot(p.astype(BF16), v)
            m_ref[h] = m_new

    @pl.when(ki < qi)
    def _():
        step(False)

    @pl.when(ki == qi)
    def _():
        step(True)
        for h in range(FOX_HEADS):
            sl = slice(h * FOX_HD, (h + 1) * FOX_HD)
            o_ref[0, :, sl] = (acc_ref[:, sl] / l_ref[h]).astype(o_ref.dtype)


def _fox_prompt(zb, fk, fv, ct_col, ct_row, tq):
    B, L, _ = fk.shape
    n = L // tq
    kv_map = lambda b, qi, ki: (b, jnp.minimum(ki, qi), 0)
    return pl.pallas_call(
        functools.partial(_fox_prompt_kernel, tq=tq),
        grid=(B, n, n),
        in_specs=[pl.BlockSpec((1, tq, HEAD_COLS), lambda b, qi, ki: (b, qi, COL_FQ)),
                  pl.BlockSpec((1, tq, HEAD_COLS), kv_map),
                  pl.BlockSpec((1, tq, HEAD_COLS), kv_map),
                  pl.BlockSpec((1, tq, FOX_HEADS), lambda b, qi, ki: (b, qi, 0)),
                  pl.BlockSpec((1, FOX_HEADS, tq), lambda b, qi, ki: (b, 0, jnp.minimum(ki, qi)))],
        out_specs=pl.BlockSpec((1, tq, HEAD_COLS), lambda b, qi, ki: (b, qi, 0)),
        out_shape=jax.ShapeDtypeStruct((B, L, HEAD_COLS), BF16),
        scratch_shapes=[pltpu.VMEM((FOX_HEADS, tq, 1), F32),
                        pltpu.VMEM((FOX_HEADS, tq, 1), F32),
                        pltpu.VMEM((tq, HEAD_COLS), F32)],
        compiler_params=_params(("parallel", "parallel", "arbitrary")),
        name="fox_prompt",
    )(zb, fk, fv, ct_col, ct_row)


def _mem_attend_kernel(q_ref, k_ref, v_ref, o_ref):
    scale = MEM_HD ** -0.5
    for h in range(MEM_HEADS):
        sl = slice(h * MEM_HD, (h + 1) * MEM_HD)
        q = q_ref[0, :, sl].astype(BF16)
        k = (k_ref[0, :, sl] * scale).astype(BF16)
        v = v_ref[0, :, sl].astype(BF16)
        s = _dot_nt(q, k)
        p = jnp.exp(s - jnp.max(s, axis=1, keepdims=True))
        o = _dot(p.astype(BF16), v) / jnp.sum(p, axis=1, keepdims=True)
        o_ref[0, :, sl] = o.astype(o_ref.dtype)


def _mem_attend(zb, mk, mv, tq, out_dtype):
    B, L, _ = zb.shape
    M = mk.shape[1]
    return pl.pallas_call(
        _mem_attend_kernel,
        grid=(B, L // tq),
        in_specs=[pl.BlockSpec((1, tq, HEAD_COLS), lambda b, i: (b, i, COL_MQ)),
                  pl.BlockSpec((1, M, HEAD_COLS), lambda b, i: (b, 0, 0)),
                  pl.BlockSpec((1, M, HEAD_COLS), lambda b, i: (b, 0, 0))],
        out_specs=pl.BlockSpec((1, tq, HEAD_COLS), lambda b, i: (b, i, 0)),
        out_shape=jax.ShapeDtypeStruct((B, L, HEAD_COLS), out_dtype),
        compiler_params=_params(("parallel", "parallel")),
        name="mem_attend",
    )(zb, mk, mv)


def _merge_kernel(x_ref, oret_ref, ofox_ref, omem_ref, gate_ref, wr_ref, wf_ref, wm_ref, wo_ref, o_ref, *, D):
    def branch(k, o_b, w_ref):
        g = jax.nn.sigmoid(gate_ref[:, k * D:(k + 1) * D].astype(F32))
        return g * _dot(o_b[...].astype(BF16), w_ref[...])

    merged = branch(0, oret_ref, wr_ref) + branch(1, ofox_ref, wf_ref) + branch(2, omem_ref, wm_ref)
    o_ref[...] = x_ref[...] + _dot(merged.astype(BF16), wo_ref[...])


def _merge(x2d, o_ret, o_fox, o_mem, zb, w_ret, w_fox, w_mem, w_o, tm):
    T, D = x2d.shape
    gate_blk = COL_GATES * HEAD_COLS // (N_BRANCH * D)
    assert gate_blk * N_BRANCH * D == COL_GATES * HEAD_COLS
    row = lambda i: (i, 0)
    const = lambda i: (0, 0)
    resident = functools.partial(pl.BlockSpec, index_map=const, pipeline_mode=pl.Buffered(1))
    return pl.pallas_call(
        functools.partial(_merge_kernel, D=D),
        grid=(T // tm,),
        in_specs=[pl.BlockSpec((tm, D), row),
                  pl.BlockSpec((tm, HEAD_COLS), row),
                  pl.BlockSpec((tm, HEAD_COLS), row),
                  pl.BlockSpec((tm, HEAD_COLS), row),
                  pl.BlockSpec((tm, N_BRANCH * D), lambda i: (i, gate_blk)),
                  resident((HEAD_COLS, D)),
                  resident((HEAD_COLS, D)),
                  resident((HEAD_COLS, D)),
                  resident((D, D))],
        out_specs=pl.BlockSpec((tm, D), row),
        out_shape=jax.ShapeDtypeStruct((T, D), F32),
        compiler_params=_params(("parallel",)),
        name="merge",
    )(x2d, o_ret, o_fox, o_mem, zb, w_ret, w_fox, w_mem, w_o)


def _mlp_kernel(x_ref, g_ref, wu_ref, wd_ref, gf_ref, o_ref, y_ref, hn_ref):
    f = pl.program_id(1)
    nf = pl.num_programs(1)

    @pl.when(f == 0)
    def _():
        x = x_ref[...]
        hn_ref[...] = _rms_normed(x, g_ref[...]).astype(BF16)
        o_ref[...] = x

    u = jnp.maximum(_dot(hn_ref[...], wu_ref[...]), 0.0)
    o_ref[...] += _dot((u * u).astype(BF16), wd_ref[...])

    @pl.when(f == nf - 1)
    def _():
        y_ref[...] = _rms_normed(o_ref[...], gf_ref[...])


def _mlp(x2d, g, w_up, w_down, g_final, tm, tf):
    T, D = x2d.shape
    Fd = w_up.shape[1]
    return pl.pallas_call(
        _mlp_kernel,
        grid=(T // tm, Fd // tf),
        in_specs=[pl.BlockSpec((tm, D), lambda i, f: (i, 0), pipeline_mode=pl.Buffered(1)),
                  pl.BlockSpec((1, D), lambda i, f: (0, 0)),
                  pl.BlockSpec((D, tf), lambda i, f: (0, f)),
                  pl.BlockSpec((tf, D), lambda i, f: (f, 0)),
                  pl.BlockSpec((1, D), lambda i, f: (0, 0))],
        out_specs=[pl.BlockSpec((tm, D), lambda i, f: (i, 0)),
                   pl.BlockSpec((tm, D), lambda i, f: (i, 0))],
        out_shape=[jax.ShapeDtypeStruct((T, D), F32), jax.ShapeDtypeStruct((T, D), F32)],
        scratch_shapes=[pltpu.VMEM((tm, D), BF16)],
        compiler_params=_params(("parallel", "arbitrary")),
        name="mlp",
    )(x2d, g.reshape(1, D), w_up, w_down, g_final.reshape(1, D))


def _fox_sample_kernel(pt_ref, q_ref, kn_ref, vn_ref, lfn_ref, *refs, G, NG):
    k_refs = refs[:G]
    v_refs = refs[G:2 * G]
    lf_refs = refs[2 * G:3 * G]
    o_ref, qall_ref, ptcol_ref, m_ref, l_ref, acc_ref, carry_ref = refs[3 * G:]
    del pt_ref
    g = pl.program_id(1)
    H = FOX_HEADS
    R = q_ref.shape[1]
    row = lax.broadcasted_iota(jnp.int32, (R, LANES), 0)
    lane = lax.broadcasted_iota(jnp.int32, (R, LANES), 1)
    same_head = ((lane - row) & (H - 1)) == 0

    def update(state, s, v):
        m_prev, l_prev, acc = state
        m_new = jnp.maximum(m_prev, jnp.max(s, axis=1, keepdims=True))
        corr = jnp.exp(m_prev - m_new)
        p = jnp.exp(s - m_new)
        l_new = l_prev * corr + jnp.sum(p, axis=1, keepdims=True)
        return m_new, l_new, acc * corr + _dot(p.astype(BF16), v)

    @pl.when(g == 0)
    def _():
        qall = (q_ref[0] * (FOX_HD ** -0.5)).astype(BF16)
        qall_ref[...] = qall
        lfn = jnp.broadcast_to(lfn_ref[0], (SUBLANES, LANES))
        ptrow = _lane_scan(lfn, H, strict_suffix=False)[0][0:1]
        ptcol = jnp.sum(jnp.where(lane == row, ptrow, 0.0), axis=1, keepdims=True)
        ptcol_ref[...] = ptcol
        zpad = jnp.zeros((LANES - R, FOX_HD), F32)
        kn = jnp.concatenate([kn_ref[0], zpad], axis=0).astype(BF16)
        vn = jnp.concatenate([vn_ref[0], zpad], axis=0).astype(BF16)
        s = _dot_nt(qall, kn) + (ptcol - ptrow)
        causal = jnp.logical_and(same_head, (lane // H) <= (row // H))
        s = jnp.where(causal, s, -jnp.inf)
        init = (jnp.full((R, 1), -jnp.inf, F32), jnp.zeros((R, 1), F32), jnp.zeros((R, FOX_HD), F32))
        m_ref[...], l_ref[...], acc_ref[...] = update(init, s, vn)
        carry_ref[...] = jnp.zeros_like(carry_ref)

    suf, rowtot = _lane_scan(jnp.concatenate([lf_refs[i][0, 0] for i in range(G)], axis=0), H,
                             strict_suffix=True)
    sub = lax.broadcasted_iota(jnp.int32, (SUBLANES, LANES), 0)
    qall = qall_ref[...]
    ptcol = ptcol_ref[...]
    carry = carry_ref[0:1, :]
    tiles = []
    tmax = None
    for i in range(G):
        pre = rowtot[i * SUBLANES:(i + 1) * SUBLANES]
        for sh in (1, 2, 4):
            pre = pre + jnp.where(sub >= sh, pltpu.roll(pre, sh, 0), 0.0)
        page_tot = pre[SUBLANES - 1:SUBLANES]
        later = suf[i * SUBLANES:(i + 1) * SUBLANES] + (page_tot - pre) + carry
        carry = carry + page_tot
        s2 = _dot_nt(qall, k_refs[i][0, 0].astype(BF16))
        for r in range(SUBLANES):
            t = jnp.where(same_head, s2[:, r * LANES:(r + 1) * LANES] + later[r:r + 1, :], -jnp.inf)
            tiles.append(t)
            tmax = t if tmax is None else jnp.maximum(tmax, t)
    carry_ref[...] = jnp.broadcast_to(carry, carry_ref.shape)
    m_prev = m_ref[...]
    m_new = jnp.maximum(m_prev, jnp.max(tmax, axis=1, keepdims=True) + ptcol)
    shift = m_new - ptcol
    corr = jnp.exp(m_prev - m_new)
    psum = jnp.zeros((R, LANES), F32)
    pv = jnp.zeros((R, FOX_HD), F32)
    for i in range(G):
        p = [jnp.exp(t - shift) for t in tiles[i * SUBLANES:(i + 1) * SUBLANES]]
        for pt in p:
            psum = psum + pt
        pv = pv + _dot(jnp.concatenate(p, axis=1).astype(BF16), v_refs[i][0, 0].astype(BF16))
    l_ref[...] = l_ref[...] * corr + jnp.sum(psum, axis=1, keepdims=True)
    acc_ref[...] = acc_ref[...] * corr + pv
    m_ref[...] = m_new

    @pl.when(g == NG - 1)
    def _():
        o_ref[0] = acc_ref[...] / l_ref[...]


def _fox_sample(fq, fk, fv, lf_new, cache_k, cache_v, cache_lf, page_table, layer, G):
    B, R, _ = fq.shape
    n_pages = page_table.shape[1]
    rows = cache_k.shape[2]
    assert rows == SUBLANES * LANES and n_pages % G == 0 and R <= LANES
    NG = n_pages // G
    tok = pl.BlockSpec((1, R, FOX_HD), lambda b, g, pt: (b, 0, 0))

    def page_spec(i, shape):
        return pl.BlockSpec((1, 1) + shape,
                            lambda b, g, pt, i=i: (layer, pt[b, n_pages - 1 - (g * G + i)], 0, 0))

    in_specs = ([tok, tok, tok, pl.BlockSpec((1, 1, LANES), lambda b, g, pt: (b, 0, 0))]
                + [page_spec(i, (rows, FOX_HD)) for i in range(G)]
                + [page_spec(i, (rows, FOX_HD)) for i in range(G)]
                + [page_spec(i, (SUBLANES, LANES)) for i in range(G)])
    grid_spec = pltpu.PrefetchScalarGridSpec(
        num_scalar_prefetch=1,
        grid=(B, NG),
        in_specs=in_specs,
        out_specs=pl.BlockSpec((1, R, FOX_HD), lambda b, g, pt: (b, 0, 0)),
        scratch_shapes=[pltpu.VMEM((R, FOX_HD), BF16),
                        pltpu.VMEM((R, 1), F32),
                        pltpu.VMEM((R, 1), F32),
                        pltpu.VMEM((R, 1), F32),
                        pltpu.VMEM((R, FOX_HD), F32),
                        pltpu.VMEM((SUBLANES, LANES), F32)])
    return pl.pallas_call(
        functools.partial(_fox_sample_kernel, G=G, NG=NG),
        grid_spec=grid_spec,
        out_shape=jax.ShapeDtypeStruct((B, R, FOX_HD), F32),
        compiler_params=_params(("parallel", "arbitrary")),
        name="fox_sample",
    )(page_table, fq, fk, fv, lf_new, *([cache_k] * G), *([cache_v] * G), *([cache_lf] * G))


def _pick(n, prefs):
    for p in prefs:
        if n % p == 0:
            return p
    return n


def _split_w_in(w_in_l, b_ff_l):
    c = HEAD_COLS
    w_b = jnp.concatenate([w_in_l[:, :5 * c], w_in_l[:, 7 * c + FOX_HEADS:]], axis=1).astype(BF16)
    w_kv = w_in_l[:, 5 * c:7 * c].astype(BF16)
    w_ff = jnp.pad(w_in_l[:, 7 * c:7 * c + FOX_HEADS], ((0, 0), (0, LANES - FOX_HEADS))).astype(BF16)
    b_ff = jnp.pad(b_ff_l.astype(F32), (0, LANES - FOX_HEADS)).reshape(1, LANES)
    return w_b, w_kv, w_ff, b_ff


def kernel(x_prompt, x_sample, mem_prompt, state_ret, cache_fox_k, cache_fox_v, cache_fox_logf,
           cache_mem_k, cache_mem_v, page_table, norm_mix, w_in, b_fox_f, gn_ret, norm_mem, w_mem_kv,
           w_br_ret, w_br_fox, w_br_mem, w_o, norm_mlp, w_up, w_down, norm_final):
    B, L, D = x_prompt.shape
    Bd, T, _ = x_sample.shape
    depth = w_in.shape[0]
    n_pool, ps = cache_fox_k.shape[1], cache_fox_k.shape[2]
    n_pages = page_table.shape[1]
    M = mem_prompt.shape[1]
    H = FOX_HEADS
    past_len = n_pages * ps
    pos_p = jnp.arange(L, dtype=jnp.int32)
    pos_s = past_len + jnp.arange(T, dtype=jnp.int32)

    tn = _pick(D, (1024, 256))
    tm_p = _pick(B * L, (1024, 512, 256))
    tm_kv = _pick(B * L, (512, 256))
    C_ret = _pick(L, (256, 128))
    tq_fox = _pick(L, (512, 256, 128))
    tq_mem = _pick(L, (512, 256, 128))
    tm_merge = _pick(B * L, (256,))
    tm_mlp = _pick(B * L, (1024, 512, 256))
    tf_mlp = _pick(w_up.shape[2], (512,))
    G = _pick(n_pages, (8, 4, 2))

    cache_k = cache_fox_k.reshape(depth, n_pool, ps * H, FOX_HD)
    cache_v = cache_fox_v.reshape(depth, n_pool, ps * H, FOX_HD)
    cache_lf = cache_fox_logf.reshape(depth, n_pool, ps * H // LANES, LANES)

    xp = x_prompt.reshape(B * L, D)
    xs = x_sample.reshape(Bd * T, D)
    outs = {k: [] for k in ("sp", "ss", "kp", "vp", "fp", "ks", "vs", "fs", "mk", "mv")}
    yp = ys = None
    for l in range(depth):
        w_b, w_kv, w_ff, b_ff = _split_w_in(w_in[l], b_fox_f[l])
        w_ret = w_br_ret[l].astype(BF16)
        w_fox = w_br_fox[l].astype(BF16)
        w_mem = w_br_mem[l].astype(BF16)
        w_o_l = w_o[l].astype(BF16)
        w_up_l = w_up[l].astype(BF16)
        w_down_l = w_down[l].astype(BF16)
        g_fin = norm_final if l == depth - 1 else jnp.ones_like(norm_final)

        zb = _norm_proj(xp, norm_mix[l], w_b, tm_p, tn, BF16)
        fk, fv, logf = _norm_kv(xp, norm_mix[l], w_kv, tm_kv, w_ff, b_ff)
        zb3 = zb.reshape(B, L, zb.shape[1])
        o_ret, s_new = _retention(zb3, pos_p, gn_ret[l], None, C_ret, BF16)
        logf3 = logf[:, :H].reshape(B, L, H)
        ct_row = _cumsum_lanes(jnp.swapaxes(logf3, 1, 2))
        o_fox = _fox_prompt(zb3, fk.reshape(B, L, HEAD_COLS), fv.reshape(B, L, HEAD_COLS),
                            jnp.swapaxes(ct_row, 1, 2), ct_row, tq_fox)
        mk, mv = _norm_kv(mem_prompt.reshape(B * M, D), norm_mem[l], w_mem_kv[l].astype(BF16),
                          _pick(B * M, (512, 256)))
        o_mem = _mem_attend(zb3, mk.reshape(B, M, HEAD_COLS), mv.reshape(B, M, HEAD_COLS), tq_mem, BF16)
        xp = _merge(xp, o_ret.reshape(B * L, HEAD_COLS), o_fox.reshape(B * L, HEAD_COLS),
                    o_mem.reshape(B * L, HEAD_COLS), zb, w_ret, w_fox, w_mem, w_o_l, tm_merge)
        xp, yp = _mlp(xp, norm_mlp[l], w_up_l, w_down_l, g_fin, tm_mlp, tf_mlp)
        outs["sp"].append(s_new)
        outs["kp"].append(fk.reshape(B, L, H, FOX_HD))
        outs["vp"].append(fv.reshape(B, L, H, FOX_HD))
        outs["fp"].append(logf3)
        outs["mk"].append(mk.reshape(B, M, MEM_HEADS, MEM_HD))
        outs["mv"].append(mv.reshape(B, M, MEM_HEADS, MEM_HD))

        zs = _norm_proj(xs, norm_mix[l], w_b, Bd * T, tn, F32)
        fk, fv, logf = _norm_kv(xs, norm_mix[l], w_kv, Bd * T, w_ff, b_ff)
        zs3 = zs.reshape(Bd, T, zs.shape[1])
        o_ret, s_new = _retention(zs3, pos_s, gn_ret[l], state_ret[l], T, F32)
        logf3 = logf[:, :H].reshape(Bd, T, H)
        fq = zs[:, COL_FQ * HEAD_COLS:(COL_FQ + 1) * HEAD_COLS]
        lf_new = jnp.pad(logf3.reshape(Bd, 1, T * H), ((0, 0), (0, 0), (0, LANES - T * H)))
        o_fox = _fox_sample(fq.reshape(Bd, T * H, FOX_HD), fk.reshape(Bd, T * H, FOX_HD),
                            fv.reshape(Bd, T * H, FOX_HD), lf_new, cache_k, cache_v, cache_lf,
                            page_table, l, G)
        o_mem = _mem_attend(zs3, cache_mem_k[l].reshape(Bd, M, HEAD_COLS),
                            cache_mem_v[l].reshape(Bd, M, HEAD_COLS), T, F32)
        xs = _merge(xs, o_ret.reshape(Bd * T, HEAD_COLS), o_fox.reshape(Bd * T, HEAD_COLS),
                    o_mem.reshape(Bd * T, HEAD_COLS), zs, w_ret, w_fox, w_mem, w_o_l, Bd * T)
        xs, ys = _mlp(xs, norm_mlp[l], w_up_l, w_down_l, g_fin, Bd * T, tf_mlp)
        outs["ss"].append(s_new)
        outs["ks"].append(fk.reshape(Bd, T, H, FOX_HD))
        outs["vs"].append(fv.reshape(Bd, T, H, FOX_HD))
        outs["fs"].append(logf3)

    st = lambda k: jnp.stack(outs[k])
    return (yp.reshape(B, L, D), ys.reshape(Bd, T, D), st("sp"), st("ss"), st("kp"), st("vp"), st("fp"),
            st("ks"), st("vs"), st("fs"), st("mk"), st("mv"))
```

```python
import functools
import math

import jax
import jax.numpy as jnp
from jax import lax
from jax.experimental import pallas as pl
from jax.experimental.pallas import tpu as pltpu

F32 = jnp.float32
BF16 = jnp.bfloat16

RET_HEADS = 8
RET_DK = 128
RET_DV = 128
FOX_HEADS = 8
FOX_HD = 128
MEM_HEADS = 4
MEM_HD = 256
N_BRANCH = 3
ROPE_BASE = 10000.0
RMS_EPS = 1e-6
GN_EPS = 1e-5

HEAD_COLS = 1024
LANES = 128
SUBLANES = 8
VMEM_LIMIT = 56 * 1024 * 1024
LOG2E = math.log2(math.e)
PAGE_RING_SLOTS = 3

COL_FQ = 4
COL_MQ = 5
COL_GATES = 6
COL_KV = 5


def _dot(a, b):
    return jnp.dot(a, b, preferred_element_type=F32)


def _dot_nt(a, b):
    return lax.dot_general(a, b, (((1,), (1,)), ((), ())), preferred_element_type=F32)


def _dot_tn(a, b):
    return lax.dot_general(a, b, (((0,), (0,)), ((), ())), preferred_element_type=F32)


def _params(sem):
    return pltpu.CompilerParams(dimension_semantics=sem, vmem_limit_bytes=VMEM_LIMIT)


def _rms_normed(x, g):
    ms = jnp.mean(x * x, axis=-1, keepdims=True)
    return x * lax.rsqrt(ms + RMS_EPS) * g


def _log_sigmoid(x):
    return jnp.minimum(x, 0.0) - jnp.log1p(jnp.exp(-jnp.abs(x)))


def _norm_proj_kernel(x_ref, g_ref, w_ref, o_ref, hn_ref):
    @pl.when(pl.program_id(1) == 0)
    def _():
        hn_ref[...] = _rms_normed(x_ref[...], g_ref[...]).astype(BF16)

    o_ref[...] = _dot(hn_ref[...], w_ref[...]).astype(o_ref.dtype)


def _norm_proj(x2d, g, w, skip_start, skip_cols, tm, tn, out_dtype):
    T, D = x2d.shape
    n_out = (w.shape[1] - skip_cols) // tn
    s0, ns = skip_start // tn, skip_cols // tn
    assert s0 * tn == skip_start and ns * tn == skip_cols and n_out * tn + skip_cols == w.shape[1]
    return pl.pallas_call(
        _norm_proj_kernel,
        grid=(T // tm, n_out),
        in_specs=[pl.BlockSpec((tm, D), lambda i, j: (i, 0), pipeline_mode=pl.Buffered(1)),
                  pl.BlockSpec((1, D), lambda i, j: (0, 0)),
                  pl.BlockSpec((D, tn), lambda i, j: (0, jnp.where(j < s0, j, j + ns)))],
        out_specs=pl.BlockSpec((tm, tn), lambda i, j: (i, j)),
        out_shape=jax.ShapeDtypeStruct((T, n_out * tn), out_dtype),
        scratch_shapes=[pltpu.VMEM((tm, D), BF16)],
        compiler_params=_params(("parallel", "arbitrary")),
        name="norm_proj",
    )(x2d, g.reshape(1, D), w)


def _regroup_cast_kernel(a_ref, b_ref, o_ref, *, n_plain, shift):
    j = pl.program_id(1)

    @pl.when(j < n_plain)
    def _():
        o_ref[...] = a_ref[...].astype(BF16)

    @pl.when(j >= n_plain)
    def _():
        ab = jnp.concatenate([a_ref[...], b_ref[...]], axis=1)
        width = ab.shape[1]
        o_ref[...] = pltpu.roll(ab, width - shift, 1)[:, :HEAD_COLS].astype(BF16)


def _regroup_cast(w_in, layer, tr):
    _, D, N = w_in.shape
    n_plain = 7
    n_out = N - FOX_HEADS
    n_blocks = pl.cdiv(n_out, HEAD_COLS)
    last = pl.cdiv(N, HEAD_COLS) - 1
    return pl.pallas_call(
        functools.partial(_regroup_cast_kernel, n_plain=n_plain, shift=FOX_HEADS),
        grid=(D // tr, n_blocks),
        in_specs=[pl.BlockSpec((None, tr, HEAD_COLS), lambda r, j: (layer, r, j)),
                  pl.BlockSpec((None, tr, HEAD_COLS),
                               lambda r, j: (layer, r, jnp.clip(j + 1, n_plain, last)))],
        out_specs=pl.BlockSpec((tr, HEAD_COLS), lambda r, j: (r, j)),
        out_shape=jax.ShapeDtypeStruct((D, n_out), BF16),
        compiler_params=_params(("parallel", "parallel")),
        name="regroup_cast",
    )(w_in, w_in)


def _norm_kv_kernel(*refs, has_ff):
    if has_ff:
        x_ref, g_ref, wk_ref, wv_ref, wff_ref, bff_ref, k_ref, v_ref, logf_ref = refs
    else:
        x_ref, g_ref, wk_ref, wv_ref, k_ref, v_ref = refs
    hn = _rms_normed(x_ref[...], g_ref[...]).astype(BF16)
    k_ref[...] = _dot(hn, wk_ref[...])
    v_ref[...] = _dot(hn, wv_ref[...])
    if has_ff:
        logf_ref[...] = _log_sigmoid(_dot(hn, wff_ref[...]) + bff_ref[...])


def _norm_kv(x2d, g, w, col_block, tm, w_ff=None, b_ff=None):
    T, D = x2d.shape
    has_ff = w_ff is not None
    row = lambda i: (i, 0)
    const = lambda i: (0, 0)
    in_specs = [pl.BlockSpec((tm, D), row), pl.BlockSpec((1, D), const),
                pl.BlockSpec((D, HEAD_COLS), lambda i: (0, col_block), pipeline_mode=pl.Buffered(1)),
                pl.BlockSpec((D, HEAD_COLS), lambda i: (0, col_block + 1), pipeline_mode=pl.Buffered(1))]
    args = [x2d, g.reshape(1, D), w, w]
    out_specs = [pl.BlockSpec((tm, HEAD_COLS), row), pl.BlockSpec((tm, HEAD_COLS), row)]
    out_shape = [jax.ShapeDtypeStruct((T, HEAD_COLS), F32), jax.ShapeDtypeStruct((T, HEAD_COLS), F32)]
    if has_ff:
        in_specs += [pl.BlockSpec((D, LANES), const), pl.BlockSpec((1, LANES), const)]
        args += [w_ff, b_ff]
        out_specs.append(pl.BlockSpec((tm, LANES), row))
        out_shape.append(jax.ShapeDtypeStruct((T, LANES), F32))
    return pl.pallas_call(
        functools.partial(_norm_kv_kernel, has_ff=has_ff),
        grid=(T // tm,),
        in_specs=in_specs,
        out_specs=out_specs,
        out_shape=out_shape,
        compiler_params=_params(("parallel",)),
        name="norm_kv",
    )(*args)


def _retention_kernel(*refs, T, Cp, has_s0):
    q_ref, k_ref, v_ref, g_ref, cos_ref, sin_ref, inner_ref, qd_ref, kd_ref, cd_ref, gn_ref = refs[:11]
    pos = 11
    if has_s0:
        s0_ref = refs[pos]
        pos += 1
    o_ref, s_out_ref, s_ref = refs[pos:pos + 3]
    c = pl.program_id(1)
    nc = pl.num_programs(1)

    @pl.when(c == 0)
    def _():
        if has_s0:
            s_ref[...] = s0_ref[0]
        else:
            s_ref[...] = jnp.zeros_like(s_ref)

    cos = cos_ref[...]
    sin = sin_ref[...]

    def pad(t):
        if T == Cp:
            return t
        return jnp.concatenate([t, jnp.zeros((Cp - T, t.shape[1]), t.dtype)], axis=0)

    for h in range(RET_HEADS):
        sl = slice(h * RET_DK, (h + 1) * RET_DK)
        q = pad(q_ref[0, :, sl].astype(F32))
        k = pad(k_ref[0, :, sl].astype(F32))
        v = pad(v_ref[0, :, sl].astype(F32)).astype(BF16)
        q = q * cos + pltpu.roll(q, RET_DK // 2, 1) * sin
        k = (k * cos + pltpu.roll(k, RET_DK // 2, 1) * sin) * (RET_DK ** -0.5)
        qb = q.astype(BF16)
        att = _dot_nt(qb, k.astype(BF16)) * inner_ref[h]
        s_prev = s_ref[h]
        o = _dot(att.astype(BF16), v) + _dot(qb, s_prev.astype(BF16)) * qd_ref[h]
        s_ref[h] = s_prev * cd_ref[h] + _dot_tn((k * kd_ref[h]).astype(BF16), v)
        mu = jnp.mean(o, axis=-1, keepdims=True)
        d = o - mu
        var = jnp.mean(d * d, axis=-1, keepdims=True)
        y = d * lax.rsqrt(var + GN_EPS) * gn_ref[:, sl]
        gate = g_ref[0, :, sl].astype(F32)
        gate = gate * jax.nn.sigmoid(gate)
        o_ref[0, :, sl] = (gate * y[:T]).astype(o_ref.dtype)

    @pl.when(c == nc - 1)
    def _():
        s_out_ref[0] = s_ref[...]


def _retention_tables(C, Cp):
    H = RET_HEADS
    lg = jnp.log1p(-jnp.exp2(-5.0 - jnp.arange(H, dtype=F32)))
    i = jnp.arange(C, dtype=F32)
    diff = i[:, None] - i[None, :]
    inner = jnp.where((diff >= 0)[None], jnp.exp(jnp.maximum(diff, 0.0)[None] * lg[:, None, None]), 0.0)
    qd = jnp.exp((i[None, :] + 1.0) * lg[:, None])
    kd = jnp.exp((C - 1.0 - i)[None, :] * lg[:, None])
    cd = jnp.exp(C * lg)
    inner = jnp.pad(inner, ((0, 0), (0, Cp - C), (0, Cp - C)))
    qd = jnp.broadcast_to(jnp.pad(qd, ((0, 0), (0, Cp - C)))[:, :, None], (H, Cp, LANES))
    kd = jnp.broadcast_to(jnp.pad(kd, ((0, 0), (0, Cp - C)))[:, :, None], (H, Cp, LANES))
    cd = jnp.broadcast_to(cd[:, None, None], (H, 1, LANES))
    return inner, qd, kd, cd


def _rope_tables(pos, rows):
    half = RET_DK // 2
    inv = ROPE_BASE ** (-jnp.arange(half, dtype=F32) / half)
    ang = pos.astype(F32)[:, None] * inv[None, :]
    cos = jnp.cos(ang)
    sin = jnp.sin(ang)
    cos_full = jnp.concatenate([cos, cos], axis=-1)
    sin_signed = jnp.concatenate([-sin, sin], axis=-1)
    padn = rows - pos.shape[0]
    return jnp.pad(cos_full, ((0, padn), (0, 0))), jnp.pad(sin_signed, ((0, padn), (0, 0)))


def _retention(z, pos, gn, s0, C, out_dtype):
    B, L, _ = z.shape
    n = L // C
    Cp = max(C, LANES)
    inner, qd, kd, cd = _retention_tables(C, Cp)
    cos, sin = _rope_tables(pos, n * Cp)
    has_s0 = s0 is not None
    H = RET_HEADS

    def seg(k):
        return pl.BlockSpec((1, C, HEAD_COLS), lambda b, c, k=k: (b, c, k))

    const3 = lambda b, c: (0, 0, 0)
    in_specs = [seg(0), seg(1), seg(2), seg(3),
                pl.BlockSpec((Cp, LANES), lambda b, c: (c, 0)),
                pl.BlockSpec((Cp, LANES), lambda b, c: (c, 0)),
                pl.BlockSpec((H, Cp, Cp), const3),
                pl.BlockSpec((H, Cp, LANES), const3),
                pl.BlockSpec((H, Cp, LANES), const3),
                pl.BlockSpec((H, 1, LANES), const3),
                pl.BlockSpec((1, HEAD_COLS), lambda b, c: (0, 0))]
    args = [z, z, z, z, cos, sin, inner, qd, kd, cd, gn.reshape(1, HEAD_COLS)]
    if has_s0:
        in_specs.append(pl.BlockSpec((1, H, RET_DK, RET_DV), lambda b, c: (b, 0, 0, 0)))
        args.append(s0)
    return pl.pallas_call(
        functools.partial(_retention_kernel, T=C, Cp=Cp, has_s0=has_s0),
        grid=(B, n),
        in_specs=in_specs,
        out_specs=[pl.BlockSpec((1, C, HEAD_COLS), lambda b, c: (b, c, 0)),
                   pl.BlockSpec((1, H, RET_DK, RET_DV), lambda b, c: (b, 0, 0, 0))],
        out_shape=[jax.ShapeDtypeStruct((B, L, HEAD_COLS), out_dtype),
                   jax.ShapeDtypeStruct((B, H, RET_DK, RET_DV), F32)],
        scratch_shapes=[pltpu.VMEM((H, RET_DK, RET_DV), F32)],
        compiler_params=_params(("parallel", "arbitrary")),
        name="retention",
    )(*args)


def _split3(x):
    hi = x.astype(BF16).astype(F32)
    r = x - hi
    mid = r.astype(BF16).astype(F32)
    lo = (r - mid).astype(BF16).astype(F32)
    return hi, mid, lo


def _lane_scan(x, stride, strict_suffix):
    n = x.shape[0]
    j = lax.broadcasted_iota(jnp.int32, (LANES, LANES), 0)
    s = lax.broadcasted_iota(jnp.int32, (LANES, LANES), 1)
    same = ((j - s) & (stride - 1)) == 0
    tri = jnp.logical_and(same, (j > s) if strict_suffix else (j <= s))
    rhs = jnp.concatenate([jnp.where(tri, 1.0, 0.0), jnp.where(same, 1.0, 0.0)], axis=1).astype(F32)
    hi, mid, lo = _split3(x)
    y = _dot(jnp.concatenate([hi, mid, lo], axis=0), rhs)
    y = y[0:n] + y[n:2 * n] + y[2 * n:3 * n]
    return y[:, :LANES], y[:, LANES:]


def _cumsum_kernel(x_ref, o_ref, *, n_chunks):
    carry = jnp.zeros((SUBLANES, LANES), F32)
    for c in range(n_chunks):
        sl = slice(c * LANES, (c + 1) * LANES)
        scan, tot = _lane_scan(x_ref[0, :, sl], 1, strict_suffix=False)
        o_ref[0, :, sl] = scan + carry
        carry = carry + tot


def _cumsum_lanes(x):
    B, H, N = x.shape
    return pl.pallas_call(
        functools.partial(_cumsum_kernel, n_chunks=N // LANES),
        grid=(B,),
        in_specs=[pl.BlockSpec((1, H, N), lambda b: (b, 0, 0))],
        out_specs=pl.BlockSpec((1, H, N), lambda b: (b, 0, 0)),
        out_shape=jax.ShapeDtypeStruct((B, H, N), F32),
        compiler_params=_params(("parallel",)),
        name="logf_cumsum",
    )(x)


def _fox_prompt_kernel(q_ref, k_ref, v_ref, cq_ref, ck_ref, o_ref, m_ref, l_ref, acc_ref, *, tq):
    qi = pl.program_id(1)
    ki = pl.program_id(2)
    kscale = (FOX_HD ** -0.5) * LOG2E

    @pl.when(ki == 0)
    def _():
        m_ref[...] = jnp.full_like(m_ref, -jnp.inf)
        l_ref[...] = jnp.zeros_like(l_ref)
        acc_ref[...] = jnp.zeros_like(acc_ref)

    def scores(h):
        sl = slice(h * FOX_HD, (h + 1) * FOX_HD)
        k = (k_ref[0, :, sl] * kscale).astype(BF16)
        return _dot_nt(q_ref[0, :, sl], k)

    def step(diagonal):
        if diagonal:
            row = lax.broadcasted_iota(jnp.int32, (tq, tq), 0)
            col = lax.broadcasted_iota(jnp.int32, (tq, tq), 1)
            causal = row >= col
        ck = ck_ref[0] * LOG2E
        s_next = scores(0)
        for h in range(FOX_HEADS):
            sl = slice(h * FOX_HD, (h + 1) * FOX_HD)
            s = s_next
            if h + 1 < FOX_HEADS:
                s_next = scores(h + 1)
            t = s - ck[h:h + 1, :]
            if diagonal:
                t = jnp.where(causal, t, -jnp.inf)
            cq = cq_ref[0, :, h:h + 1] * LOG2E
            m_prev = m_ref[h]
            m_new = jnp.maximum(m_prev, jnp.max(t, axis=1, keepdims=True) + cq)
            corr = jnp.exp2(m_prev - m_new)
            p = jnp.exp2(t - (m_new - cq))
            l_ref[h] = l_ref[h] * corr + jnp.sum(p, axis=1, keepdims=True)
            v = v_ref[0, :, sl].astype(BF16)
            acc_ref[:, sl] = acc_ref[:, sl] * corr + _dot(p.astype(BF16), v)
            m_ref[h] = m_new

    @pl.when(ki < qi)
    def _():
        step(False)

    @pl.when(ki == qi)
    def _():
        step(True)
        for h in range(FOX_HEADS):
            sl = slice(h * FOX_HD, (h + 1) * FOX_HD)
            o_ref[0, :, sl] = (acc_ref[:, sl] / l_ref[h]).astype(o_ref.dtype)


def _fox_prompt(zb, fk, fv, ct_col, ct_row, tq):
    B, L, _ = fk.shape
    n = L // tq
    kv_map = lambda b, qi, ki: (b, jnp.minimum(ki, qi), 0)
    return pl.pallas_call(
        functools.partial(_fox_prompt_kernel, tq=tq),
        grid=(B, n, n),
        in_specs=[pl.BlockSpec((1, tq, HEAD_COLS), lambda b, qi, ki: (b, qi, COL_FQ)),
                  pl.BlockSpec((1, tq, HEAD_COLS), kv_map),
                  pl.BlockSpec((1, tq, HEAD_COLS), kv_map),
                  pl.BlockSpec((1, tq, FOX_HEADS), lambda b, qi, ki: (b, qi, 0)),
                  pl.BlockSpec((1, FOX_HEADS, tq), lambda b, qi, ki: (b, 0, jnp.minimum(ki, qi)))],
        out_specs=pl.BlockSpec((1, tq, HEAD_COLS), lambda b, qi, ki: (b, qi, 0)),
        out_shape=jax.ShapeDtypeStruct((B, L, HEAD_COLS), BF16),
        scratch_shapes=[pltpu.VMEM((FOX_HEADS, tq, 1), F32),
                        pltpu.VMEM((FOX_HEADS, tq, 1), F32),
                        pltpu.VMEM((tq, HEAD_COLS), F32)],
        compiler_params=_params(("parallel", "parallel", "arbitrary")),
        name="fox_prompt",
    )(zb, fk, fv, ct_col, ct_row)


def _mem_attend_kernel(q_ref, k_ref, v_ref, o_ref):
    scale = MEM_HD ** -0.5
    for h in range(MEM_HEADS):
        sl = slice(h * MEM_HD, (h + 1) * MEM_HD)
        q = q_ref[0, :, sl].astype(BF16)
        k = (k_ref[0, :, sl] * scale).astype(BF16)
        v = v_ref[0, :, sl].astype(BF16)
        s = _dot_nt(q, k)
        p = jnp.exp(s - jnp.max(s, axis=1, keepdims=True))
        o = _dot(p.astype(BF16), v) / jnp.sum(p, axis=1, keepdims=True)
        o_ref[0, :, sl] = o.astype(o_ref.dtype)


def _mem_attend(zb, mk, mv, tq, out_dtype):
    B, L, _ = zb.shape
    M = mk.shape[1]
    return pl.pallas_call(
        _mem_attend_kernel,
        grid=(B, L // tq),
        in_specs=[pl.BlockSpec((1, tq, HEAD_COLS), lambda b, i: (b, i, COL_MQ)),
                  pl.BlockSpec((1, M, HEAD_COLS), lambda b, i: (b, 0, 0)),
                  pl.BlockSpec((1, M, HEAD_COLS), lambda b, i: (b, 0, 0))],
        out_specs=pl.BlockSpec((1, tq, HEAD_COLS), lambda b, i: (b, i, 0)),
        out_shape=jax.ShapeDtypeStruct((B, L, HEAD_COLS), out_dtype),
        compiler_params=_params(("parallel", "parallel")),
        name="mem_attend",
    )(zb, mk, mv)


def _merge_kernel(x_ref, oret_ref, ofox_ref, omem_ref, gate_ref, wr_ref, wf_ref, wm_ref, wo_ref, o_ref, *, D):
    def branch(k, o_b, w_ref):
        g = jax.nn.sigmoid(gate_ref[:, k * D:(k + 1) * D].astype(F32))
        return g * _dot(o_b[...].astype(BF16), w_ref[...])

    merged = branch(0, oret_ref, wr_ref) + branch(1, ofox_ref, wf_ref) + branch(2, omem_ref, wm_ref)
    o_ref[...] = x_ref[...] + _dot(merged.astype(BF16), wo_ref[...])


def _merge(x2d, o_ret, o_fox, o_mem, zb, w_ret, w_fox, w_mem, w_o, tm):
    T, D = x2d.shape
    gate_blk = COL_GATES * HEAD_COLS // (N_BRANCH * D)
    assert gate_blk * N_BRANCH * D == COL_GATES * HEAD_COLS
    row = lambda i: (i, 0)
    const = lambda i: (0, 0)
    resident = functools.partial(pl.BlockSpec, index_map=const, pipeline_mode=pl.Buffered(1))
    return pl.pallas_call(
        functools.partial(_merge_kernel, D=D),
        grid=(T // tm,),
        in_specs=[pl.BlockSpec((tm, D), row),
                  pl.BlockSpec((tm, HEAD_COLS), row),
                  pl.BlockSpec((tm, HEAD_COLS), row),
                  pl.BlockSpec((tm, HEAD_COLS), row),
                  pl.BlockSpec((tm, N_BRANCH * D), lambda i: (i, gate_blk)),
                  resident((HEAD_COLS, D)),
                  resident((HEAD_COLS, D)),
                  resident((HEAD_COLS, D)),
                  resident((D, D))],
        out_specs=pl.BlockSpec((tm, D), row),
        out_shape=jax.ShapeDtypeStruct((T, D), F32),
        compiler_params=_params(("parallel",)),
        name="merge",
    )(x2d, o_ret, o_fox, o_mem, zb, w_ret, w_fox, w_mem, w_o)


def _mlp_kernel(x_ref, g_ref, wu_ref, wd_ref, gf_ref, o_ref, hn_ref, *, final):
    f = pl.program_id(1)
    nf = pl.num_programs(1)

    @pl.when(f == 0)
    def _():
        x = x_ref[...]
        hn_ref[...] = _rms_normed(x, g_ref[...]).astype(BF16)
        o_ref[...] = x

    u = jnp.maximum(_dot(hn_ref[...], wu_ref[...]), 0.0)
    o_ref[...] += _dot((u * u).astype(BF16), wd_ref[...])

    if final:
        @pl.when(f == nf - 1)
        def _():
            o_ref[...] = _rms_normed(o_ref[...], gf_ref[...])


def _mlp(x2d, g, w_up, w_down, g_final, tm, tf, final):
    T, D = x2d.shape
    Fd = w_up.shape[1]
    return pl.pallas_call(
        functools.partial(_mlp_kernel, final=final),
        grid=(T // tm, Fd // tf),
        in_specs=[pl.BlockSpec((tm, D), lambda i, f: (i, 0), pipeline_mode=pl.Buffered(1)),
                  pl.BlockSpec((1, D), lambda i, f: (0, 0)),
                  pl.BlockSpec((D, tf), lambda i, f: (0, f)),
                  pl.BlockSpec((tf, D), lambda i, f: (f, 0)),
                  pl.BlockSpec((1, D), lambda i, f: (0, 0))],
        out_specs=pl.BlockSpec((tm, D), lambda i, f: (i, 0)),
        out_shape=jax.ShapeDtypeStruct((T, D), F32),
        scratch_shapes=[pltpu.VMEM((tm, D), BF16)],
        compiler_params=_params(("parallel", "arbitrary")),
        name="mlp",
    )(x2d, g.reshape(1, D), w_up, w_down, g_final.reshape(1, D))


def _fox_sample_kernel(pt_ref, q_ref, kn_ref, vn_ref, lfn_ref, kc_ref, vc_ref, *refs, G, NG, layer):
    lf_refs = refs[:G]
    o_ref, kbuf, vbuf, sem, qall_ref, ptcol_ref, m_ref, l_ref, acc_ref, carry_ref = refs[G:]
    g = pl.program_id(1)
    H = FOX_HEADS
    n_pages = NG * G
    n_slots = kbuf.shape[0]
    step = pl.program_id(0) * NG + g
    n_steps = pl.num_programs(0) * NG

    def page_copies(s, slot):
        bb = lax.div(s, NG)
        gg = lax.rem(s, NG)
        out = []
        for i in range(G):
            page = pt_ref[bb, n_pages - 1 - (gg * G + i)]
            out.append(pltpu.make_async_copy(kc_ref.at[layer, page], kbuf.at[slot, i], sem.at[0, slot]))
            out.append(pltpu.make_async_copy(vc_ref.at[layer, page], vbuf.at[slot, i], sem.at[1, slot]))
        return out

    @pl.when(step == 0)
    def _():
        for j in range(n_slots - 1):
            for c in page_copies(j, j):
                c.start()

    ahead = step + (n_slots - 1)

    @pl.when(ahead < n_steps)
    def _():
        for c in page_copies(ahead, lax.rem(ahead, n_slots)):
            c.start()

    slot = lax.rem(step, n_slots)
    for c in page_copies(step, slot):
        c.wait()
    k_pages = [kbuf.at[slot, i] for i in range(G)]
    v_pages = [vbuf.at[slot, i] for i in range(G)]
    R = q_ref.shape[1]
    row = lax.broadcasted_iota(jnp.int32, (R, LANES), 0)
    lane = lax.broadcasted_iota(jnp.int32, (R, LANES), 1)
    same_head = ((lane - row) & (H - 1)) == 0

    def update(state, s, v):
        m_prev, l_prev, acc = state
        m_new = jnp.maximum(m_prev, jnp.max(s, axis=1, keepdims=True))
        corr = jnp.exp(m_prev - m_new)
        p = jnp.exp(s - m_new)
        l_new = l_prev * corr + jnp.sum(p, axis=1, keepdims=True)
        return m_new, l_new, acc * corr + _dot(p.astype(BF16), v)

    @pl.when(g == 0)
    def _():
        qall = (q_ref[0] * (FOX_HD ** -0.5)).astype(BF16)
        qall_ref[...] = qall
        lfn = jnp.broadcast_to(lfn_ref[0], (SUBLANES, LANES))
        ptrow = _lane_scan(lfn, H, strict_suffix=False)[0][0:1]
        ptcol = jnp.sum(jnp.where(lane == row, ptrow, 0.0), axis=1, keepdims=True)
        ptcol_ref[...] = ptcol
        zpad = jnp.zeros((LANES - R, FOX_HD), F32)
        kn = jnp.concatenate([kn_ref[0], zpad], axis=0).astype(BF16)
        vn = jnp.concatenate([vn_ref[0], zpad], axis=0).astype(BF16)
        s = _dot_nt(qall, kn) + (ptcol - ptrow)
        causal = jnp.logical_and(same_head, (lane // H) <= (row // H))
        s = jnp.where(causal, s, -jnp.inf)
        init = (jnp.full((R, 1), -jnp.inf, F32), jnp.zeros((R, 1), F32), jnp.zeros((R, FOX_HD), F32))
        m_ref[...], l_ref[...], acc_ref[...] = update(init, s, vn)
        carry_ref[...] = jnp.zeros_like(carry_ref)

    suf, rowtot = _lane_scan(jnp.concatenate([lf_refs[i][0, 0] for i in range(G)], axis=0), H,
                             strict_suffix=True)
    sub = lax.broadcasted_iota(jnp.int32, (SUBLANES, LANES), 0)
    qall = qall_ref[...]
    ptcol = ptcol_ref[...]
    carry = carry_ref[0:1, :]
    tiles = []
    tmax = None
    for i in range(G):
        pre = rowtot[i * SUBLANES:(i + 1) * SUBLANES]
        for sh in (1, 2, 4):
            pre = pre + jnp.where(sub >= sh, pltpu.roll(pre, sh, 0), 0.0)
        page_tot = pre[SUBLANES - 1:SUBLANES]
        later = suf[i * SUBLANES:(i + 1) * SUBLANES] + (page_tot - pre) + carry
        carry = carry + page_tot
        s2 = _dot_nt(qall, k_pages[i][...].astype(BF16))
        for r in range(SUBLANES):
            t = jnp.where(same_head, s2[:, r * LANES:(r + 1) * LANES] + later[r:r + 1, :], -jnp.inf)
            tiles.append(t)
            tmax = t if tmax is None else jnp.maximum(tmax, t)
    carry_ref[...] = jnp.broadcast_to(carry, carry_ref.shape)
    m_prev = m_ref[...]
    m_new = jnp.maximum(m_prev, jnp.max(tmax, axis=1, keepdims=True) + ptcol)
    shift = m_new - ptcol
    corr = jnp.exp(m_prev - m_new)
    psum = jnp.zeros((R, LANES), F32)
    pv = jnp.zeros((R, FOX_HD), F32)
    for i in range(G):
        p = [jnp.exp(t - shift) for t in tiles[i * SUBLANES:(i + 1) * SUBLANES]]
        for pt in p:
            psum = psum + pt
        pv = pv + _dot(jnp.concatenate(p, axis=1).astype(BF16), v_pages[i][...].astype(BF16))
    l_ref[...] = l_ref[...] * corr + jnp.sum(psum, axis=1, keepdims=True)
    acc_ref[...] = acc_ref[...] * corr + pv
    m_ref[...] = m_new

    @pl.when(g == NG - 1)
    def _():
        o_ref[0] = acc_ref[...] / l_ref[...]


def _fox_sample(fq, fk, fv, lf_new, cache_k, cache_v, cache_lf, page_table, layer, G):
    B, R, _ = fq.shape
    n_pages = page_table.shape[1]
    rows = cache_k.shape[2]
    assert rows == SUBLANES * LANES and n_pages % G == 0 and R <= LANES
    NG = n_pages // G
    tok = pl.BlockSpec((1, R, FOX_HD), lambda b, g, pt: (b, 0, 0))

    def lf_spec(i):
        return pl.BlockSpec((1, 1, SUBLANES, LANES),
                            lambda b, g, pt, i=i: (layer, pt[b, n_pages - 1 - (g * G + i)], 0, 0))

    assert B * NG >= PAGE_RING_SLOTS - 1
    in_specs = ([tok, tok, tok, pl.BlockSpec((1, 1, LANES), lambda b, g, pt: (b, 0, 0)),
                 pl.BlockSpec(memory_space=pl.ANY), pl.BlockSpec(memory_space=pl.ANY)]
                + [lf_spec(i) for i in range(G)])
    grid_spec = pltpu.PrefetchScalarGridSpec(
        num_scalar_prefetch=1,
        grid=(B, NG),
        in_specs=in_specs,
        out_specs=pl.BlockSpec((1, R, FOX_HD), lambda b, g, pt: (b, 0, 0)),
        scratch_shapes=[pltpu.VMEM((PAGE_RING_SLOTS, G, rows, FOX_HD), F32),
                        pltpu.VMEM((PAGE_RING_SLOTS, G, rows, FOX_HD), F32),
                        pltpu.SemaphoreType.DMA((2, PAGE_RING_SLOTS)),
                        pltpu.VMEM((R, FOX_HD), BF16),
                        pltpu.VMEM((R, 1), F32),
                        pltpu.VMEM((R, 1), F32),
                        pltpu.VMEM((R, 1), F32),
                        pltpu.VMEM((R, FOX_HD), F32),
                        pltpu.VMEM((SUBLANES, LANES), F32)])
    return pl.pallas_call(
        functools.partial(_fox_sample_kernel, G=G, NG=NG, layer=layer),
        grid_spec=grid_spec,
        out_shape=jax.ShapeDtypeStruct((B, R, FOX_HD), F32),
        compiler_params=_params(("arbitrary", "arbitrary")),
        name="fox_sample",
    )(page_table, fq, fk, fv, lf_new, cache_k, cache_v, *([cache_lf] * G))


def _pick(n, prefs):
    for p in prefs:
        if n % p == 0:
            return p
    return n


def _split_w_in(w_in, b_ff_l, layer):
    c = HEAD_COLS
    w_all = _regroup_cast(w_in, layer, _pick(w_in.shape[1], (512, 256)))
    w_ff = jnp.pad(w_in[layer, :, 7 * c:7 * c + FOX_HEADS], ((0, 0), (0, LANES - FOX_HEADS))).astype(BF16)
    b_ff = jnp.pad(b_ff_l.astype(F32), (0, LANES - FOX_HEADS)).reshape(1, LANES)
    return w_all, w_ff, b_ff


def kernel(x_prompt, x_sample, mem_prompt, state_ret, cache_fox_k, cache_fox_v, cache_fox_logf,
           cache_mem_k, cache_mem_v, page_table, norm_mix, w_in, b_fox_f, gn_ret, norm_mem, w_mem_kv,
           w_br_ret, w_br_fox, w_br_mem, w_o, norm_mlp, w_up, w_down, norm_final):
    B, L, D = x_prompt.shape
    Bd, T, _ = x_sample.shape
    depth = w_in.shape[0]
    n_pool, ps = cache_fox_k.shape[1], cache_fox_k.shape[2]
    n_pages = page_table.shape[1]
    M = mem_prompt.shape[1]
    H = FOX_HEADS
    past_len = n_pages * ps
    pos_p = jnp.arange(L, dtype=jnp.int32)
    pos_s = past_len + jnp.arange(T, dtype=jnp.int32)

    tn = _pick(D, (1024, 256))
    tm_p = _pick(B * L, (1024, 512, 256))
    tm_kv = _pick(B * L, (512, 256))
    C_ret = _pick(L, (256, 128))
    tq_fox = _pick(L, (512, 256, 128))
    tq_mem = _pick(L, (512, 256, 128))
    tm_merge = _pick(B * L, (256,))
    tm_mlp = _pick(B * L, (1024, 512, 256))
    tf_mlp = _pick(w_up.shape[2], (512,))
    G = _pick(n_pages, (8, 4, 2))

    cache_k = cache_fox_k.reshape(depth, n_pool, ps * H, FOX_HD)
    cache_v = cache_fox_v.reshape(depth, n_pool, ps * H, FOX_HD)
    cache_lf = cache_fox_logf.reshape(depth, n_pool, ps * H // LANES, LANES)

    xp = x_prompt.reshape(B * L, D)
    xs = x_sample.reshape(Bd * T, D)
    outs = {k: [] for k in ("sp", "ss", "kp", "vp", "fp", "ks", "vs", "fs", "mk", "mv")}
    for l in range(depth):
        last = l == depth - 1
        w_all, w_ff, b_ff = _split_w_in(w_in, b_fox_f[l], l)
        kv_cols = (COL_KV * HEAD_COLS, 2 * HEAD_COLS)
        w_ret = w_br_ret[l].astype(BF16)
        w_fox = w_br_fox[l].astype(BF16)
        w_mem = w_br_mem[l].astype(BF16)
        w_o_l = w_o[l].astype(BF16)
        w_up_l = w_up[l].astype(BF16)
        w_down_l = w_down[l].astype(BF16)

        zb = _norm_proj(xp, norm_mix[l], w_all, *kv_cols, tm_p, tn, BF16)
        fk, fv, logf = _norm_kv(xp, norm_mix[l], w_all, COL_KV, tm_kv, w_ff, b_ff)
        zb3 = zb.reshape(B, L, zb.shape[1])
        o_ret, s_new = _retention(zb3, pos_p, gn_ret[l], None, C_ret, BF16)
        logf3 = logf[:, :H].reshape(B, L, H)
        ct_row = _cumsum_lanes(jnp.swapaxes(logf3, 1, 2))
        o_fox = _fox_prompt(zb3, fk.reshape(B, L, HEAD_COLS), fv.reshape(B, L, HEAD_COLS),
                            jnp.swapaxes(ct_row, 1, 2), ct_row, tq_fox)
        mk, mv = _norm_kv(mem_prompt.reshape(B * M, D), norm_mem[l], w_mem_kv[l].astype(BF16), 0,
                          _pick(B * M, (512, 256)))
        o_mem = _mem_attend(zb3, mk.reshape(B, M, HEAD_COLS), mv.reshape(B, M, HEAD_COLS), tq_mem, BF16)
        xp = _merge(xp, o_ret.reshape(B * L, HEAD_COLS), o_fox.reshape(B * L, HEAD_COLS),
                    o_mem.reshape(B * L, HEAD_COLS), zb, w_ret, w_fox, w_mem, w_o_l, tm_merge)
        xp = _mlp(xp, norm_mlp[l], w_up_l, w_down_l, norm_final, tm_mlp, tf_mlp, last)
        outs["sp"].append(s_new)
        outs["kp"].append(fk.reshape(B, L, H, FOX_HD))
        outs["vp"].append(fv.reshape(B, L, H, FOX_HD))
        outs["fp"].append(logf3)
        outs["mk"].append(mk.reshape(B, M, MEM_HEADS, MEM_HD))
        outs["mv"].append(mv.reshape(B, M, MEM_HEADS, MEM_HD))

        zs = _norm_proj(xs, norm_mix[l], w_all, *kv_cols, Bd * T, tn, F32)
        fk, fv, logf = _norm_kv(xs, norm_mix[l], w_all, COL_KV, Bd * T, w_ff, b_ff)
        zs3 = zs.reshape(Bd, T, zs.shape[1])
        o_ret, s_new = _retention(zs3, pos_s, gn_ret[l], state_ret[l], T, F32)
        logf3 = logf[:, :H].reshape(Bd, T, H)
        fq = zs[:, COL_FQ * HEAD_COLS:(COL_FQ + 1) * HEAD_COLS]
        lf_new = jnp.pad(logf3.reshape(Bd, 1, T * H), ((0, 0), (0, 0), (0, LANES - T * H)))
        o_fox = _fox_sample(fq.reshape(Bd, T * H, FOX_HD), fk.reshape(Bd, T * H, FOX_HD),
                            fv.reshape(Bd, T * H, FOX_HD), lf_new, cache_k, cache_v, cache_lf,
                            page_table, l, G)
        o_mem = _mem_attend(zs3, cache_mem_k[l].reshape(Bd, M, HEAD_COLS),
                            cache_mem_v[l].reshape(Bd, M, HEAD_COLS), T, F32)
        xs = _merge(xs, o_ret.reshape(Bd * T, HEAD_COLS), o_fox.reshape(Bd * T, HEAD_COLS),
                    o_mem.reshape(Bd * T, HEAD_COLS), zs, w_ret, w_fox, w_mem, w_o_l, Bd * T)
        xs = _mlp(xs, norm_mlp[l], w_up_l, w_down_l, norm_final, Bd * T, tf_mlp, last)
        outs["ss"].append(s_new)
        outs["ks"].append(fk.reshape(Bd, T, H, FOX_HD))
        outs["vs"].append(fv.reshape(Bd, T, H, FOX_HD))
        outs["fs"].append(logf3)

    st = lambda k: jnp.stack(outs[k])
    return (xp.reshape(B, L, D), xs.reshape(Bd, T, D), st("sp"), st("ss"), st("kp"), st("vp"), st("fp"),
            st("ks"), st("vs"), st("fs"), st("mk"), st("mv"))
```

```python
import functools
import math

import jax
import jax.numpy as jnp
from jax import lax
from jax.experimental import pallas as pl
from jax.experimental.pallas import tpu as pltpu

F32 = jnp.float32
BF16 = jnp.bfloat16

RET_HEADS = 8
RET_DK = 128
RET_DV = 128
FOX_HEADS = 8
FOX_HD = 128
MEM_HEADS = 4
MEM_HD = 256
N_BRANCH = 3
ROPE_BASE = 10000.0
RMS_EPS = 1e-6
GN_EPS = 1e-5

HEAD_COLS = 1024
LANES = 128
SUBLANES = 8
VMEM_LIMIT = 56 * 1024 * 1024
LOG2E = math.log2(math.e)
PAGE_RING_SLOTS = 3

COL_FQ = 4
COL_MQ = 5
COL_GATES = 6
COL_KV = 5


def _dot(a, b):
    return jnp.dot(a, b, preferred_element_type=F32)


def _dot_nt(a, b):
    return lax.dot_general(a, b, (((1,), (1,)), ((), ())), preferred_element_type=F32)


def _dot_tn(a, b):
    return lax.dot_general(a, b, (((0,), (0,)), ((), ())), preferred_element_type=F32)


def _params(sem):
    return pltpu.CompilerParams(dimension_semantics=sem, vmem_limit_bytes=VMEM_LIMIT)


def _rms_normed(x, g):
    ms = jnp.mean(x * x, axis=-1, keepdims=True)
    return x * lax.rsqrt(ms + RMS_EPS) * g


def _log_sigmoid(x):
    return jnp.minimum(x, 0.0) - jnp.log1p(jnp.exp(-jnp.abs(x)))


def _norm_proj_kernel(x_ref, g_ref, w_ref, o_ref, hn_ref):
    @pl.when(pl.program_id(1) == 0)
    def _():
        hn_ref[...] = _rms_normed(x_ref[...], g_ref[...]).astype(BF16)

    o_ref[...] = _dot(hn_ref[...], w_ref[...]).astype(o_ref.dtype)


def _norm_proj(x2d, g, w, skip_start, skip_cols, tm, tn, out_dtype):
    T, D = x2d.shape
    n_out = (w.shape[1] - skip_cols) // tn
    s0, ns = skip_start // tn, skip_cols // tn
    assert s0 * tn == skip_start and ns * tn == skip_cols and n_out * tn + skip_cols == w.shape[1]
    return pl.pallas_call(
        _norm_proj_kernel,
        grid=(T // tm, n_out),
        in_specs=[pl.BlockSpec((tm, D), lambda i, j: (i, 0)),
                  pl.BlockSpec((1, D), lambda i, j: (0, 0)),
                  pl.BlockSpec((D, tn), lambda i, j: (0, jnp.where(j < s0, j, j + ns)))],
        out_specs=pl.BlockSpec((tm, tn), lambda i, j: (i, j)),
        out_shape=jax.ShapeDtypeStruct((T, n_out * tn), out_dtype),
        scratch_shapes=[pltpu.VMEM((tm, D), BF16)],
        compiler_params=_params(("parallel", "arbitrary")),
        name="norm_proj",
    )(x2d, g.reshape(1, D), w)


def _regroup_cast_kernel(a_ref, b_ref, o_ref, *, n_plain, shift):
    j = pl.program_id(1)

    @pl.when(j < n_plain)
    def _():
        o_ref[...] = a_ref[...].T.astype(BF16)

    @pl.when(j >= n_plain)
    def _():
        x = jnp.concatenate([a_ref[shift:, :], b_ref[:shift, :]], axis=0)
        o_ref[...] = x.T.astype(BF16)


def _regroup_cast(w_t, layer, tr):
    _, N, D = w_t.shape
    n_plain = 7
    n_out = N - FOX_HEADS
    n_blocks = pl.cdiv(n_out, HEAD_COLS)
    last = pl.cdiv(N, HEAD_COLS) - 1
    return pl.pallas_call(
        functools.partial(_regroup_cast_kernel, n_plain=n_plain, shift=FOX_HEADS),
        grid=(D // tr, n_blocks),
        in_specs=[pl.BlockSpec((None, HEAD_COLS, tr), lambda r, j: (layer, j, r)),
                  pl.BlockSpec((None, HEAD_COLS, tr),
                               lambda r, j: (layer, jnp.clip(j + 1, n_plain, last), r))],
        out_specs=pl.BlockSpec((tr, HEAD_COLS), lambda r, j: (r, j)),
        out_shape=jax.ShapeDtypeStruct((D, n_out), BF16),
        compiler_params=_params(("parallel", "parallel")),
        name="regroup_cast",
    )(w_t, w_t)


def _norm_kv_kernel(*refs, has_ff):
    if has_ff:
        x_ref, g_ref, wk_ref, wv_ref, wff_ref, bff_ref, k_ref, v_ref, logf_ref = refs
    else:
        x_ref, g_ref, wk_ref, wv_ref, k_ref, v_ref = refs
    hn = _rms_normed(x_ref[...], g_ref[...]).astype(BF16)
    k_ref[...] = _dot(hn, wk_ref[...])
    v_ref[...] = _dot(hn, wv_ref[...])
    if has_ff:
        logf_ref[...] = _log_sigmoid(_dot(hn, wff_ref[...]) + bff_ref[...])


def _norm_kv(x2d, g, w, col_block, tm, w_ff=None, b_ff=None):
    T, D = x2d.shape
    has_ff = w_ff is not None
    row = lambda i: (i, 0)
    const = lambda i: (0, 0)
    in_specs = [pl.BlockSpec((tm, D), row), pl.BlockSpec((1, D), const),
                pl.BlockSpec((D, HEAD_COLS), lambda i: (0, col_block), pipeline_mode=pl.Buffered(1)),
                pl.BlockSpec((D, HEAD_COLS), lambda i: (0, col_block + 1), pipeline_mode=pl.Buffered(1))]
    args = [x2d, g.reshape(1, D), w, w]
    out_specs = [pl.BlockSpec((tm, HEAD_COLS), row), pl.BlockSpec((tm, HEAD_COLS), row)]
    out_shape = [jax.ShapeDtypeStruct((T, HEAD_COLS), F32), jax.ShapeDtypeStruct((T, HEAD_COLS), F32)]
    if has_ff:
        in_specs += [pl.BlockSpec((D, LANES), const), pl.BlockSpec((1, LANES), const)]
        args += [w_ff, b_ff]
        out_specs.append(pl.BlockSpec((tm, LANES), row))
        out_shape.append(jax.ShapeDtypeStruct((T, LANES), F32))
    return pl.pallas_call(
        functools.partial(_norm_kv_kernel, has_ff=has_ff),
        grid=(T // tm,),
        in_specs=in_specs,
        out_specs=out_specs,
        out_shape=out_shape,
        compiler_params=_params(("parallel",)),
        name="norm_kv",
    )(*args)


def _retention_kernel(*refs, T, Cp, has_s0):
    q_ref, k_ref, v_ref, g_ref, cos_ref, sin_ref, inner_ref, qd_ref, kd_ref, cd_ref, gn_ref = refs[:11]
    pos = 11
    if has_s0:
        s0_ref = refs[pos]
        pos += 1
    o_ref, s_out_ref, s_ref = refs[pos:pos + 3]
    c = pl.program_id(1)
    nc = pl.num_programs(1)

    @pl.when(c == 0)
    def _():
        if has_s0:
            s_ref[...] = s0_ref[0]
        else:
            s_ref[...] = jnp.zeros_like(s_ref)

    cos = cos_ref[...]
    sin = sin_ref[...]

    def pad(t):
        if T == Cp:
            return t
        return jnp.concatenate([t, jnp.zeros((Cp - T, t.shape[1]), t.dtype)], axis=0)

    for h in range(RET_HEADS):
        sl = slice(h * RET_DK, (h + 1) * RET_DK)
        q = pad(q_ref[0, :, sl].astype(F32))
        k = pad(k_ref[0, :, sl].astype(F32))
        v = pad(v_ref[0, :, sl].astype(F32)).astype(BF16)
        q = q * cos + pltpu.roll(q, RET_DK // 2, 1) * sin
        k = (k * cos + pltpu.roll(k, RET_DK // 2, 1) * sin) * (RET_DK ** -0.5)
        qb = q.astype(BF16)
        att = _dot_nt(qb, k.astype(BF16)) * inner_ref[h]
        s_prev = s_ref[h]
        o = _dot(att.astype(BF16), v) + _dot(qb, s_prev.astype(BF16)) * qd_ref[h]
        s_ref[h] = s_prev * cd_ref[h] + _dot_tn((k * kd_ref[h]).astype(BF16), v)
        mu = jnp.mean(o, axis=-1, keepdims=True)
        d = o - mu
        var = jnp.mean(d * d, axis=-1, keepdims=True)
        y = d * lax.rsqrt(var + GN_EPS) * gn_ref[:, sl]
        gate = g_ref[0, :, sl].astype(F32)
        gate = gate * jax.nn.sigmoid(gate)
        o_ref[0, :, sl] = (gate * y[:T]).astype(o_ref.dtype)

    @pl.when(c == nc - 1)
    def _():
        s_out_ref[0] = s_ref[...]


def _retention_tables(C, Cp):
    H = RET_HEADS
    lg = jnp.log1p(-jnp.exp2(-5.0 - jnp.arange(H, dtype=F32)))
    i = jnp.arange(C, dtype=F32)
    diff = i[:, None] - i[None, :]
    inner = jnp.where((diff >= 0)[None], jnp.exp(jnp.maximum(diff, 0.0)[None] * lg[:, None, None]), 0.0)
    qd = jnp.exp((i[None, :] + 1.0) * lg[:, None])
    kd = jnp.exp((C - 1.0 - i)[None, :] * lg[:, None])
    cd = jnp.exp(C * lg)
    inner = jnp.pad(inner, ((0, 0), (0, Cp - C), (0, Cp - C)))
    qd = jnp.broadcast_to(jnp.pad(qd, ((0, 0), (0, Cp - C)))[:, :, None], (H, Cp, LANES))
    kd = jnp.broadcast_to(jnp.pad(kd, ((0, 0), (0, Cp - C)))[:, :, None], (H, Cp, LANES))
    cd = jnp.broadcast_to(cd[:, None, None], (H, 1, LANES))
    return inner, qd, kd, cd


def _rope_tables(pos, rows):
    half = RET_DK // 2
    inv = ROPE_BASE ** (-jnp.arange(half, dtype=F32) / half)
    ang = pos.astype(F32)[:, None] * inv[None, :]
    cos = jnp.cos(ang)
    sin = jnp.sin(ang)
    cos_full = jnp.concatenate([cos, cos], axis=-1)
    sin_signed = jnp.concatenate([-sin, sin], axis=-1)
    padn = rows - pos.shape[0]
    return jnp.pad(cos_full, ((0, padn), (0, 0))), jnp.pad(sin_signed, ((0, padn), (0, 0)))


def _retention(z, pos, gn, s0, C, out_dtype):
    B, L, _ = z.shape
    n = L // C
    Cp = max(C, LANES)
    inner, qd, kd, cd = _retention_tables(C, Cp)
    cos, sin = _rope_tables(pos, n * Cp)
    has_s0 = s0 is not None
    H = RET_HEADS

    def seg(k):
        return pl.BlockSpec((1, C, HEAD_COLS), lambda b, c, k=k: (b, c, k))

    const3 = lambda b, c: (0, 0, 0)
    in_specs = [seg(0), seg(1), seg(2), seg(3),
                pl.BlockSpec((Cp, LANES), lambda b, c: (c, 0)),
                pl.BlockSpec((Cp, LANES), lambda b, c: (c, 0)),
                pl.BlockSpec((H, Cp, Cp), const3),
                pl.BlockSpec((H, Cp, LANES), const3),
                pl.BlockSpec((H, Cp, LANES), const3),
                pl.BlockSpec((H, 1, LANES), const3),
                pl.BlockSpec((1, HEAD_COLS), lambda b, c: (0, 0))]
    args = [z, z, z, z, cos, sin, inner, qd, kd, cd, gn.reshape(1, HEAD_COLS)]
    if has_s0:
        in_specs.append(pl.BlockSpec((1, H, RET_DK, RET_DV), lambda b, c: (b, 0, 0, 0)))
        args.append(s0)
    return pl.pallas_call(
        functools.partial(_retention_kernel, T=C, Cp=Cp, has_s0=has_s0),
        grid=(B, n),
        in_specs=in_specs,
        out_specs=[pl.BlockSpec((1, C, HEAD_COLS), lambda b, c: (b, c, 0)),
                   pl.BlockSpec((1, H, RET_DK, RET_DV), lambda b, c: (b, 0, 0, 0))],
        out_shape=[jax.ShapeDtypeStruct((B, L, HEAD_COLS), out_dtype),
                   jax.ShapeDtypeStruct((B, H, RET_DK, RET_DV), F32)],
        scratch_shapes=[pltpu.VMEM((H, RET_DK, RET_DV), F32)],
        compiler_params=_params(("parallel", "arbitrary")),
        name="retention",
    )(*args)


def _split3(x):
    hi = x.astype(BF16).astype(F32)
    r = x - hi
    mid = r.astype(BF16).astype(F32)
    lo = (r - mid).astype(BF16).astype(F32)
    return hi, mid, lo


def _lane_scan(x, stride, strict_suffix):
    n = x.shape[0]
    j = lax.broadcasted_iota(jnp.int32, (LANES, LANES), 0)
    s = lax.broadcasted_iota(jnp.int32, (LANES, LANES), 1)
    same = ((j - s) & (stride - 1)) == 0
    tri = jnp.logical_and(same, (j > s) if strict_suffix else (j <= s))
    rhs = jnp.concatenate([jnp.where(tri, 1.0, 0.0), jnp.where(same, 1.0, 0.0)], axis=1).astype(F32)
    hi, mid, lo = _split3(x)
    y = _dot(jnp.concatenate([hi, mid, lo], axis=0), rhs)
    y = y[0:n] + y[n:2 * n] + y[2 * n:3 * n]
    return y[:, :LANES], y[:, LANES:]


def _cumsum_kernel(x_ref, o_ref, *, n_chunks):
    carry = jnp.zeros((SUBLANES, LANES), F32)
    for c in range(n_chunks):
        sl = slice(c * LANES, (c + 1) * LANES)
        scan, tot = _lane_scan(x_ref[0, :, sl], 1, strict_suffix=False)
        o_ref[0, :, sl] = scan + carry
        carry = carry + tot


def _cumsum_lanes(x):
    B, H, N = x.shape
    return pl.pallas_call(
        functools.partial(_cumsum_kernel, n_chunks=N // LANES),
        grid=(B,),
        in_specs=[pl.BlockSpec((1, H, N), lambda b: (b, 0, 0))],
        out_specs=pl.BlockSpec((1, H, N), lambda b: (b, 0, 0)),
        out_shape=jax.ShapeDtypeStruct((B, H, N), F32),
        compiler_params=_params(("parallel",)),
        name="logf_cumsum",
    )(x)


def _fox_prompt_kernel(q_ref, k_ref, v_ref, cq_ref, ck_ref, o_ref, m_ref, l_ref, acc_ref, *, tq):
    qi = pl.program_id(1)
    ki = pl.program_id(2)
    kscale = (FOX_HD ** -0.5) * LOG2E

    @pl.when(ki == 0)
    def _():
        m_ref[...] = jnp.full_like(m_ref, -jnp.inf)
        l_ref[...] = jnp.zeros_like(l_ref)
        acc_ref[...] = jnp.zeros_like(acc_ref)

    def scores(h):
        sl = slice(h * FOX_HD, (h + 1) * FOX_HD)
        k = (k_ref[0, :, sl] * kscale).astype(BF16)
        return _dot_nt(q_ref[0, :, sl], k)

    def step(diagonal):
        if diagonal:
            row = lax.broadcasted_iota(jnp.int32, (tq, tq), 0)
            col = lax.broadcasted_iota(jnp.int32, (tq, tq), 1)
            causal = row >= col
        ck = ck_ref[0] * LOG2E
        s_next = scores(0)
        for h in range(FOX_HEADS):
            sl = slice(h * FOX_HD, (h + 1) * FOX_HD)
            s = s_next
            if h + 1 < FOX_HEADS:
                s_next = scores(h + 1)
            t = s - ck[h:h + 1, :]
            if diagonal:
                t = jnp.where(causal, t, -jnp.inf)
            cq = cq_ref[0, :, h:h + 1] * LOG2E
            m_prev = m_ref[h]
            m_new = jnp.maximum(m_prev, jnp.max(t, axis=1, keepdims=True) + cq)
            corr = jnp.exp2(m_prev - m_new)
            p = jnp.exp2(t - (m_new - cq))
            psum = p[:, :LANES]
            for c in range(1, tq // LANES):
                psum = psum + p[:, c * LANES:(c + 1) * LANES]
            l_ref[h] = l_ref[h] * corr + psum
            v = v_ref[0, :, sl].astype(BF16)
            acc_ref[:, sl] = acc_ref[:, sl] * corr + _dot(p.astype(BF16), v)
            m_ref[h] = m_new

    @pl.when(ki < qi)
    def _():
        step(False)

    @pl.when(ki == qi)
    def _():
        step(True)
        for h in range(FOX_HEADS):
            sl = slice(h * FOX_HD, (h + 1) * FOX_HD)
            l = jnp.sum(l_ref[h], axis=1, keepdims=True)
            o_ref[0, :, sl] = (acc_ref[:, sl] / l).astype(o_ref.dtype)


def _fox_prompt(zb, fk, fv, ct_col, ct_row, tq):
    B, L, _ = fk.shape
    n = L // tq
    kv_map = lambda b, qi, ki: (b, jnp.minimum(ki, qi), 0)
    return pl.pallas_call(
        functools.partial(_fox_prompt_kernel, tq=tq),
        grid=(B, n, n),
        in_specs=[pl.BlockSpec((1, tq, HEAD_COLS), lambda b, qi, ki: (b, qi, COL_FQ)),
                  pl.BlockSpec((1, tq, HEAD_COLS), kv_map),
                  pl.BlockSpec((1, tq, HEAD_COLS), kv_map),
                  pl.BlockSpec((1, tq, FOX_HEADS), lambda b, qi, ki: (b, qi, 0)),
                  pl.BlockSpec((1, FOX_HEADS, tq), lambda b, qi, ki: (b, 0, jnp.minimum(ki, qi)))],
        out_specs=pl.BlockSpec((1, tq, HEAD_COLS), lambda b, qi, ki: (b, qi, 0)),
        out_shape=jax.ShapeDtypeStruct((B, L, HEAD_COLS), BF16),
        scratch_shapes=[pltpu.VMEM((FOX_HEADS, tq, 1), F32),
                        pltpu.VMEM((FOX_HEADS, tq, LANES), F32),
                        pltpu.VMEM((tq, HEAD_COLS), F32)],
        compiler_params=_params(("parallel", "parallel", "arbitrary")),
        name="fox_prompt",
    )(zb, fk, fv, ct_col, ct_row)


def _mem_attend_kernel(q_ref, k_ref, v_ref, o_ref):
    scale = MEM_HD ** -0.5
    for h in range(MEM_HEADS):
        sl = slice(h * MEM_HD, (h + 1) * MEM_HD)
        q = q_ref[0, :, sl].astype(BF16)
        k = (k_ref[0, :, sl] * scale).astype(BF16)
        v = v_ref[0, :, sl].astype(BF16)
        s = _dot_nt(q, k)
        p = jnp.exp(s - jnp.max(s, axis=1, keepdims=True))
        o = _dot(p.astype(BF16), v) / jnp.sum(p, axis=1, keepdims=True)
        o_ref[0, :, sl] = o.astype(o_ref.dtype)


def _mem_attend(zb, mk, mv, tq, out_dtype):
    B, L, _ = zb.shape
    M = mk.shape[1]
    return pl.pallas_call(
        _mem_attend_kernel,
        grid=(B, L // tq),
        in_specs=[pl.BlockSpec((1, tq, HEAD_COLS), lambda b, i: (b, i, COL_MQ)),
                  pl.BlockSpec((1, M, HEAD_COLS), lambda b, i: (b, 0, 0)),
                  pl.BlockSpec((1, M, HEAD_COLS), lambda b, i: (b, 0, 0))],
        out_specs=pl.BlockSpec((1, tq, HEAD_COLS), lambda b, i: (b, i, 0)),
        out_shape=jax.ShapeDtypeStruct((B, L, HEAD_COLS), out_dtype),
        compiler_params=_params(("parallel", "parallel")),
        name="mem_attend",
    )(zb, mk, mv)


def _merge_kernel(x_ref, oret_ref, ofox_ref, omem_ref, gate_ref, wr_ref, wf_ref, wm_ref, wo_ref, o_ref, *, D):
    def branch(k, o_b, w_ref):
        g = jax.nn.sigmoid(gate_ref[:, k * D:(k + 1) * D].astype(F32))
        return g * _dot(o_b[...].astype(BF16), w_ref[...])

    merged = branch(0, oret_ref, wr_ref) + branch(1, ofox_ref, wf_ref) + branch(2, omem_ref, wm_ref)
    o_ref[...] = x_ref[...] + _dot(merged.astype(BF16), wo_ref[...])


def _merge(x2d, o_ret, o_fox, o_mem, zb, w_ret, w_fox, w_mem, w_o, tm):
    T, D = x2d.shape
    gate_blk = COL_GATES * HEAD_COLS // (N_BRANCH * D)
    assert gate_blk * N_BRANCH * D == COL_GATES * HEAD_COLS
    row = lambda i: (i, 0)
    const = lambda i: (0, 0)
    resident = functools.partial(pl.BlockSpec, index_map=const, pipeline_mode=pl.Buffered(1))
    return pl.pallas_call(
        functools.partial(_merge_kernel, D=D),
        grid=(T // tm,),
        in_specs=[pl.BlockSpec((tm, D), row),
                  pl.BlockSpec((tm, HEAD_COLS), row),
                  pl.BlockSpec((tm, HEAD_COLS), row),
                  pl.BlockSpec((tm, HEAD_COLS), row),
                  pl.BlockSpec((tm, N_BRANCH * D), lambda i: (i, gate_blk)),
                  resident((HEAD_COLS, D)),
                  resident((HEAD_COLS, D)),
                  resident((HEAD_COLS, D)),
                  resident((D, D))],
        out_specs=pl.BlockSpec((tm, D), row),
        out_shape=jax.ShapeDtypeStruct((T, D), F32),
        compiler_params=_params(("parallel",)),
        name="merge",
    )(x2d, o_ret, o_fox, o_mem, zb, w_ret, w_fox, w_mem, w_o)


def _mlp_kernel(x_ref, g_ref, wu_ref, wd_ref, gf_ref, o_ref, hn_ref, *, final):
    f = pl.program_id(1)
    nf = pl.num_programs(1)

    @pl.when(f == 0)
    def _():
        x = x_ref[...]
        hn_ref[...] = _rms_normed(x, g_ref[...]).astype(BF16)
        o_ref[...] = x

    u = jnp.maximum(_dot(hn_ref[...], wu_ref[...]), 0.0)
    o_ref[...] += _dot((u * u).astype(BF16), wd_ref[...])

    if final:
        @pl.when(f == nf - 1)
        def _():
            o_ref[...] = _rms_normed(o_ref[...], gf_ref[...])


def _mlp(x2d, g, w_up, w_down, g_final, tm, tf, final):
    T, D = x2d.shape
    Fd = w_up.shape[1]
    return pl.pallas_call(
        functools.partial(_mlp_kernel, final=final),
        grid=(T // tm, Fd // tf),
        in_specs=[pl.BlockSpec((tm, D), lambda i, f: (i, 0)),
                  pl.BlockSpec((1, D), lambda i, f: (0, 0)),
                  pl.BlockSpec((D, tf), lambda i, f: (0, f)),
                  pl.BlockSpec((tf, D), lambda i, f: (f, 0)),
                  pl.BlockSpec((1, D), lambda i, f: (0, 0))],
        out_specs=pl.BlockSpec((tm, D), lambda i, f: (i, 0)),
        out_shape=jax.ShapeDtypeStruct((T, D), F32),
        scratch_shapes=[pltpu.VMEM((tm, D), BF16)],
        compiler_params=_params(("parallel", "arbitrary")),
        name="mlp",
    )(x2d, g.reshape(1, D), w_up, w_down, g_final.reshape(1, D))


def _fox_sample_kernel(pt_ref, q_ref, kn_ref, vn_ref, lfn_ref, kc_ref, vc_ref, *refs, G, NG, layer):
    lf_refs = refs[:G]
    o_ref, kbuf, vbuf, sem, qall_ref, ptcol_ref, m_ref, l_ref, acc_ref, carry_ref = refs[G:]
    g = pl.program_id(1)
    H = FOX_HEADS
    n_pages = NG * G
    n_slots = kbuf.shape[0]
    step = pl.program_id(0) * NG + g
    n_steps = pl.num_programs(0) * NG

    def page_copies(s, slot):
        bb = lax.div(s, NG)
        gg = lax.rem(s, NG)
        out = []
        for i in range(G):
            page = pt_ref[bb, n_pages - 1 - (gg * G + i)]
            out.append(pltpu.make_async_copy(kc_ref.at[layer, page], kbuf.at[slot, i], sem.at[0, slot]))
            out.append(pltpu.make_async_copy(vc_ref.at[layer, page], vbuf.at[slot, i], sem.at[1, slot]))
        return out

    @pl.when(step == 0)
    def _():
        for j in range(n_slots - 1):
            for c in page_copies(j, j):
                c.start()

    ahead = step + (n_slots - 1)

    @pl.when(ahead < n_steps)
    def _():
        for c in page_copies(ahead, lax.rem(ahead, n_slots)):
            c.start()

    slot = lax.rem(step, n_slots)
    for c in page_copies(step, slot):
        c.wait()
    k_pages = [kbuf.at[slot, i] for i in range(G)]
    v_pages = [vbuf.at[slot, i] for i in range(G)]
    R = q_ref.shape[1]
    row = lax.broadcasted_iota(jnp.int32, (R, LANES), 0)
    lane = lax.broadcasted_iota(jnp.int32, (R, LANES), 1)
    same_head = ((lane - row) & (H - 1)) == 0

    def update(state, s, v):
        m_prev, l_prev, acc = state
        m_new = jnp.maximum(m_prev, jnp.max(s, axis=1, keepdims=True))
        corr = jnp.exp(m_prev - m_new)
        p = jnp.exp(s - m_new)
        l_new = l_prev * corr + jnp.sum(p, axis=1, keepdims=True)
        return m_new, l_new, acc * corr + _dot(p.astype(BF16), v)

    @pl.when(g == 0)
    def _():
        qall = (q_ref[0] * (FOX_HD ** -0.5)).astype(BF16)
        qall_ref[...] = qall
        lfn = jnp.broadcast_to(lfn_ref[0], (SUBLANES, LANES))
        ptrow = _lane_scan(lfn, H, strict_suffix=False)[0][0:1]
        ptcol = jnp.sum(jnp.where(lane == row, ptrow, 0.0), axis=1, keepdims=True)
        ptcol_ref[...] = ptcol
        zpad = jnp.zeros((LANES - R, FOX_HD), F32)
        kn = jnp.concatenate([kn_ref[0], zpad], axis=0).astype(BF16)
        vn = jnp.concatenate([vn_ref[0], zpad], axis=0).astype(BF16)
        s = _dot_nt(qall, kn) + (ptcol - ptrow)
        causal = jnp.logical_and(same_head, (lane // H) <= (row // H))
        s = jnp.where(causal, s, -jnp.inf)
        init = (jnp.full((R, 1), -jnp.inf, F32), jnp.zeros((R, 1), F32), jnp.zeros((R, FOX_HD), F32))
        m_ref[...], l_ref[...], acc_ref[...] = update(init, s, vn)
        carry_ref[...] = jnp.zeros_like(carry_ref)

    suf, rowtot = _lane_scan(jnp.concatenate([lf_refs[i][0, 0] for i in range(G)], axis=0), H,
                             strict_suffix=True)
    sub = lax.broadcasted_iota(jnp.int32, (SUBLANES, LANES), 0)
    qall = qall_ref[...]
    ptcol = ptcol_ref[...]
    carry = carry_ref[0:1, :]
    tiles = []
    tmax = None
    for i in range(G):
        pre = rowtot[i * SUBLANES:(i + 1) * SUBLANES]
        for sh in (1, 2, 4):
            pre = pre + jnp.where(sub >= sh, pltpu.roll(pre, sh, 0), 0.0)
        page_tot = pre[SUBLANES - 1:SUBLANES]
        later = suf[i * SUBLANES:(i + 1) * SUBLANES] + (page_tot - pre) + carry
        carry = carry + page_tot
        s2 = _dot_nt(qall, k_pages[i][...].astype(BF16))
        for r in range(SUBLANES):
            t = jnp.where(same_head, s2[:, r * LANES:(r + 1) * LANES] + later[r:r + 1, :], -jnp.inf)
            tiles.append(t)
            tmax = t if tmax is None else jnp.maximum(tmax, t)
    carry_ref[...] = jnp.broadcast_to(carry, carry_ref.shape)
    m_prev = m_ref[...]
    m_new = jnp.maximum(m_prev, jnp.max(tmax, axis=1, keepdims=True) + ptcol)
    shift = m_new - ptcol
    corr = jnp.exp(m_prev - m_new)
    psum = jnp.zeros((R, LANES), F32)
    pv = jnp.zeros((R, FOX_HD), F32)
    for i in range(G):
        p = [jnp.exp(t - shift) for t in tiles[i * SUBLANES:(i + 1) * SUBLANES]]
        for pt in p:
            psum = psum + pt
        pv = pv + _dot(jnp.concatenate(p, axis=1).astype(BF16), v_pages[i][...].astype(BF16))
    l_ref[...] = l_ref[...] * corr + jnp.sum(psum, axis=1, keepdims=True)
    acc_ref[...] = acc_ref[...] * corr + pv
    m_ref[...] = m_new

    @pl.when(g == NG - 1)
    def _():
        o_ref[0] = acc_ref[...] / l_ref[...]


def _fox_sample(fq, fk, fv, lf_new, cache_k, cache_v, cache_lf, page_table, layer, G):
    B, R, _ = fq.shape
    n_pages = page_table.shape[1]
    rows = cache_k.shape[2]
    assert rows == SUBLANES * LANES and n_pages % G == 0 and R <= LANES
    NG = n_pages // G
    tok = pl.BlockSpec((1, R, FOX_HD), lambda b, g, pt: (b, 0, 0))

    def lf_spec(i):
        return pl.BlockSpec((1, 1, SUBLANES, LANES),
                            lambda b, g, pt, i=i: (layer, pt[b, n_pages - 1 - (g * G + i)], 0, 0))

    assert B * NG >= PAGE_RING_SLOTS - 1
    in_specs = ([tok, tok, tok, pl.BlockSpec((1, 1, LANES), lambda b, g, pt: (b, 0, 0)),
                 pl.BlockSpec(memory_space=pl.ANY), pl.BlockSpec(memory_space=pl.ANY)]
                + [lf_spec(i) for i in range(G)])
    grid_spec = pltpu.PrefetchScalarGridSpec(
        num_scalar_prefetch=1,
        grid=(B, NG),
        in_specs=in_specs,
        out_specs=pl.BlockSpec((1, R, FOX_HD), lambda b, g, pt: (b, 0, 0)),
        scratch_shapes=[pltpu.VMEM((PAGE_RING_SLOTS, G, rows, FOX_HD), F32),
                        pltpu.VMEM((PAGE_RING_SLOTS, G, rows, FOX_HD), F32),
                        pltpu.SemaphoreType.DMA((2, PAGE_RING_SLOTS)),
                        pltpu.VMEM((R, FOX_HD), BF16),
                        pltpu.VMEM((R, 1), F32),
                        pltpu.VMEM((R, 1), F32),
                        pltpu.VMEM((R, 1), F32),
                        pltpu.VMEM((R, FOX_HD), F32),
                        pltpu.VMEM((SUBLANES, LANES), F32)])
    return pl.pallas_call(
        functools.partial(_fox_sample_kernel, G=G, NG=NG, layer=layer),
        grid_spec=grid_spec,
        out_shape=jax.ShapeDtypeStruct((B, R, FOX_HD), F32),
        compiler_params=_params(("arbitrary", "arbitrary")),
        name="fox_sample",
    )(page_table, fq, fk, fv, lf_new, cache_k, cache_v, *([cache_lf] * G))


def _pick(n, prefs):
    for p in prefs:
        if n % p == 0:
            return p
    return n


def _split_w_in(w_in, b_ff_l, layer):
    c = HEAD_COLS
    w_all = _regroup_cast(jnp.swapaxes(w_in, 1, 2), layer, _pick(w_in.shape[1], (512, 256)))
    w_ff = jnp.pad(w_in[layer, :, 7 * c:7 * c + FOX_HEADS], ((0, 0), (0, LANES - FOX_HEADS))).astype(BF16)
    b_ff = jnp.pad(b_ff_l.astype(F32), (0, LANES - FOX_HEADS)).reshape(1, LANES)
    return w_all, w_ff, b_ff


def kernel(x_prompt, x_sample, mem_prompt, state_ret, cache_fox_k, cache_fox_v, cache_fox_logf,
           cache_mem_k, cache_mem_v, page_table, norm_mix, w_in, b_fox_f, gn_ret, norm_mem, w_mem_kv,
           w_br_ret, w_br_fox, w_br_mem, w_o, norm_mlp, w_up, w_down, norm_final):
    B, L, D = x_prompt.shape
    Bd, T, _ = x_sample.shape
    depth = w_in.shape[0]
    n_pool, ps = cache_fox_k.shape[1], cache_fox_k.shape[2]
    n_pages = page_table.shape[1]
    M = mem_prompt.shape[1]
    H = FOX_HEADS
    past_len = n_pages * ps
    pos_p = jnp.arange(L, dtype=jnp.int32)
    pos_s = past_len + jnp.arange(T, dtype=jnp.int32)

    tn = _pick(D, (1024, 256))
    tm_p = _pick(B * L, (1024, 512, 256))
    tm_kv = _pick(B * L, (512, 256))
    C_ret = _pick(L, (256, 128))
    tq_fox = _pick(L, (512, 256, 128))
    tq_mem = _pick(L, (512, 256, 128))
    tm_merge = _pick(B * L, (256,))
    tm_mlp = _pick(B * L, (1024, 512, 256))
    tf_mlp = _pick(w_up.shape[2], (512,))
    G = _pick(n_pages, (8, 4, 2))

    cache_k = cache_fox_k.reshape(depth, n_pool, ps * H, FOX_HD)
    cache_v = cache_fox_v.reshape(depth, n_pool, ps * H, FOX_HD)
    cache_lf = cache_fox_logf.reshape(depth, n_pool, ps * H // LANES, LANES)

    xp = x_prompt.reshape(B * L, D)
    xs = x_sample.reshape(Bd * T, D)
    outs = {k: [] for k in ("sp", "ss", "kp", "vp", "fp", "ks", "vs", "fs", "mk", "mv")}
    for l in range(depth):
        last = l == depth - 1
        w_all, w_ff, b_ff = _split_w_in(w_in, b_fox_f[l], l)
        kv_cols = (COL_KV * HEAD_COLS, 2 * HEAD_COLS)
        w_ret = w_br_ret[l].astype(BF16)
        w_fox = w_br_fox[l].astype(BF16)
        w_mem = w_br_mem[l].astype(BF16)
        w_o_l = w_o[l].astype(BF16)
        w_up_l = w_up[l].astype(BF16)
        w_down_l = w_down[l].astype(BF16)

        zb = _norm_proj(xp, norm_mix[l], w_all, *kv_cols, tm_p, tn, BF16)
        fk, fv, logf = _norm_kv(xp, norm_mix[l], w_all, COL_KV, tm_kv, w_ff, b_ff)
        zb3 = zb.reshape(B, L, zb.shape[1])
        o_ret, s_new = _retention(zb3, pos_p, gn_ret[l], None, C_ret, BF16)
        logf3 = logf[:, :H].reshape(B, L, H)
        ct_row = _cumsum_lanes(jnp.swapaxes(logf3, 1, 2))
        o_fox = _fox_prompt(zb3, fk.reshape(B, L, HEAD_COLS), fv.reshape(B, L, HEAD_COLS),
                            jnp.swapaxes(ct_row, 1, 2), ct_row, tq_fox)
        mk, mv = _norm_kv(mem_prompt.reshape(B * M, D), norm_mem[l], w_mem_kv[l].astype(BF16), 0,
                          _pick(B * M, (512, 256)))
        o_mem = _mem_attend(zb3, mk.reshape(B, M, HEAD_COLS), mv.reshape(B, M, HEAD_COLS), tq_mem, BF16)
        xp = _merge(xp, o_ret.reshape(B * L, HEAD_COLS), o_fox.reshape(B * L, HEAD_COLS),
                    o_mem.reshape(B * L, HEAD_COLS), zb, w_ret, w_fox, w_mem, w_o_l, tm_merge)
        xp = _mlp(xp, norm_mlp[l], w_up_l, w_down_l, norm_final, tm_mlp, tf_mlp, last)
        outs["sp"].append(s_new)
        outs["kp"].append(fk.reshape(B, L, H, FOX_HD))
        outs["vp"].append(fv.reshape(B, L, H, FOX_HD))
        outs["fp"].append(logf3)
        outs["mk"].append(mk.reshape(B, M, MEM_HEADS, MEM_HD))
        outs["mv"].append(mv.reshape(B, M, MEM_HEADS, MEM_HD))

        zs = _norm_proj(xs, norm_mix[l], w_all, *kv_cols, Bd * T, tn, F32)
        fk, fv, logf = _norm_kv(xs, norm_mix[l], w_all, COL_KV, Bd * T, w_ff, b_ff)
        zs3 = zs.reshape(Bd, T, zs.shape[1])
        o_ret, s_new = _retention(zs3, pos_s, gn_ret[l], state_ret[l], T, F32)
        logf3 = logf[:, :H].reshape(Bd, T, H)
        fq = zs[:, COL_FQ * HEAD_COLS:(COL_FQ + 1) * HEAD_COLS]
        lf_new = jnp.pad(logf3.reshape(Bd, 1, T * H), ((0, 0), (0, 0), (0, LANES - T * H)))
        o_fox = _fox_sample(fq.reshape(Bd, T * H, FOX_HD), fk.reshape(Bd, T * H, FOX_HD),
                            fv.reshape(Bd, T * H, FOX_HD), lf_new, cache_k, cache_v, cache_lf,
                            page_table, l, G)
        o_mem = _mem_attend(zs3, cache_mem_k[l].reshape(Bd, M, HEAD_COLS),
                            cache_mem_v[l].reshape(Bd, M, HEAD_COLS), T, F32)
        xs = _merge(xs, o_ret.reshape(Bd * T, HEAD_COLS), o_fox.reshape(Bd * T, HEAD_COLS),
                    o_mem.reshape(Bd * T, HEAD_COLS), zs, w_ret, w_fox, w_mem, w_o_l, Bd * T)
        xs = _mlp(xs, norm_mlp[l], w_up_l, w_down_l, norm_final, Bd * T, tf_mlp, last)
        outs["ss"].append(s_new)
        outs["ks"].append(fk.reshape(Bd, T, H, FOX_HD))
        outs["vs"].append(fv.reshape(Bd, T, H, FOX_HD))
        outs["fs"].append(logf3)

    st = lambda k: jnp.stack(outs[k])
    return (xp.reshape(B, L, D), xs.reshape(Bd, T, D), st("sp"), st("ss"), st("kp"), st("vp"), st("fp"),
            st("ks"), st("vs"), st("fs"), st("mk"), st("mv"))
```

```python
import functools
import math

import jax
import jax.numpy as jnp
from jax import lax
from jax.experimental import pallas as pl
from jax.experimental.pallas import tpu as pltpu

F32 = jnp.float32
BF16 = jnp.bfloat16

RET_HEADS = 8
RET_DK = 128
RET_DV = 128
FOX_HEADS = 8
FOX_HD = 128
MEM_HEADS = 4
MEM_HD = 256
N_BRANCH = 3
ROPE_BASE = 10000.0
RMS_EPS = 1e-6
GN_EPS = 1e-5

HEAD_COLS = 1024
LANES = 128
SUBLANES = 8
VMEM_LIMIT = 56 * 1024 * 1024
LOG2E = math.log2(math.e)
PAGE_RING_SLOTS = 3
MLP_FOX_RING_SLOTS = 2

COL_FQ = 4
COL_MQ = 5
COL_GATES = 6
COL_KV = 5


def _dot(a, b):
    return jnp.dot(a, b, preferred_element_type=F32)


def _dot_nt(a, b):
    return lax.dot_general(a, b, (((1,), (1,)), ((), ())), preferred_element_type=F32)


def _dot_tn(a, b):
    return lax.dot_general(a, b, (((0,), (0,)), ((), ())), preferred_element_type=F32)


def _params(sem):
    return pltpu.CompilerParams(dimension_semantics=sem, vmem_limit_bytes=VMEM_LIMIT)


def _rms_normed(x, g):
    ms = jnp.mean(x * x, axis=-1, keepdims=True)
    return x * lax.rsqrt(ms + RMS_EPS) * g


def _log_sigmoid(x):
    return jnp.minimum(x, 0.0) - jnp.log1p(jnp.exp(-jnp.abs(x)))


def _norm_proj_kernel(x_ref, g_ref, w_ref, o_ref, hn_ref):
    @pl.when(pl.program_id(1) == 0)
    def _():
        hn_ref[...] = _rms_normed(x_ref[...], g_ref[...]).astype(BF16)

    o_ref[...] = _dot(hn_ref[...], w_ref[...]).astype(o_ref.dtype)


def _norm_proj(x2d, g, w, skip_start, skip_cols, tm, tn, out_dtype):
    T, D = x2d.shape
    n_out = (w.shape[1] - skip_cols) // tn
    s0, ns = skip_start // tn, skip_cols // tn
    assert s0 * tn == skip_start and ns * tn == skip_cols and n_out * tn + skip_cols == w.shape[1]
    return pl.pallas_call(
        _norm_proj_kernel,
        grid=(T // tm, n_out),
        in_specs=[pl.BlockSpec((tm, D), lambda i, j: (i, 0)),
                  pl.BlockSpec((1, D), lambda i, j: (0, 0)),
                  pl.BlockSpec((D, tn), lambda i, j: (0, jnp.where(j < s0, j, j + ns)))],
        out_specs=pl.BlockSpec((tm, tn), lambda i, j: (i, j)),
        out_shape=jax.ShapeDtypeStruct((T, n_out * tn), out_dtype),
        scratch_shapes=[pltpu.VMEM((tm, D), BF16)],
        compiler_params=_params(("parallel", "arbitrary")),
        name="norm_proj",
    )(x2d, g.reshape(1, D), w)


def _regroup_cast_kernel(a_ref, b_ref, o_ref, *, n_plain, shift):
    j = pl.program_id(1)

    @pl.when(j < n_plain)
    def _():
        o_ref[...] = a_ref[...].T.astype(BF16)

    @pl.when(j >= n_plain)
    def _():
        x = jnp.concatenate([a_ref[shift:, :], b_ref[:shift, :]], axis=0)
        o_ref[...] = x.T.astype(BF16)


def _regroup_cast(w_t, layer, tr):
    _, N, D = w_t.shape
    n_plain = 7
    n_out = N - FOX_HEADS
    n_blocks = pl.cdiv(n_out, HEAD_COLS)
    last = pl.cdiv(N, HEAD_COLS) - 1
    return pl.pallas_call(
        functools.partial(_regroup_cast_kernel, n_plain=n_plain, shift=FOX_HEADS),
        grid=(D // tr, n_blocks),
        in_specs=[pl.BlockSpec((None, HEAD_COLS, tr), lambda r, j: (layer, j, r)),
                  pl.BlockSpec((None, HEAD_COLS, tr),
                               lambda r, j: (layer, jnp.clip(j + 1, n_plain, last), r))],
        out_specs=pl.BlockSpec((tr, HEAD_COLS), lambda r, j: (r, j)),
        out_shape=jax.ShapeDtypeStruct((D, n_out), BF16),
        compiler_params=_params(("parallel", "parallel")),
        name="regroup_cast",
    )(w_t, w_t)


def _norm_kv_kernel(*refs, has_ff):
    if has_ff:
        x_ref, g_ref, wk_ref, wv_ref, wff_ref, bff_ref, k_ref, v_ref, logf_ref = refs
    else:
        x_ref, g_ref, wk_ref, wv_ref, k_ref, v_ref = refs
    hn = _rms_normed(x_ref[...], g_ref[...]).astype(BF16)
    k_ref[...] = _dot(hn, wk_ref[...])
    v_ref[...] = _dot(hn, wv_ref[...])
    if has_ff:
        logf_ref[...] = _log_sigmoid(_dot(hn, wff_ref[...]) + bff_ref[...])


def _norm_kv(x2d, g, w, col_block, tm, w_ff=None, b_ff=None):
    T, D = x2d.shape
    has_ff = w_ff is not None
    row = lambda i: (i, 0)
    const = lambda i: (0, 0)
    in_specs = [pl.BlockSpec((tm, D), row), pl.BlockSpec((1, D), const),
                pl.BlockSpec((D, HEAD_COLS), lambda i: (0, col_block), pipeline_mode=pl.Buffered(1)),
                pl.BlockSpec((D, HEAD_COLS), lambda i: (0, col_block + 1), pipeline_mode=pl.Buffered(1))]
    args = [x2d, g.reshape(1, D), w, w]
    out_specs = [pl.BlockSpec((tm, HEAD_COLS), row), pl.BlockSpec((tm, HEAD_COLS), row)]
    out_shape = [jax.ShapeDtypeStruct((T, HEAD_COLS), F32), jax.ShapeDtypeStruct((T, HEAD_COLS), F32)]
    if has_ff:
        in_specs += [pl.BlockSpec((D, LANES), const), pl.BlockSpec((1, LANES), const)]
        args += [w_ff, b_ff]
        out_specs.append(pl.BlockSpec((tm, LANES), row))
        out_shape.append(jax.ShapeDtypeStruct((T, LANES), F32))
    return pl.pallas_call(
        functools.partial(_norm_kv_kernel, has_ff=has_ff),
        grid=(T // tm,),
        in_specs=in_specs,
        out_specs=out_specs,
        out_shape=out_shape,
        compiler_params=_params(("parallel",)),
        name="norm_kv",
    )(*args)


def _retention_kernel(*refs, T, Cp, has_s0):
    q_ref, k_ref, v_ref, g_ref, cos_ref, sin_ref, inner_ref, qd_ref, kd_ref, cd_ref, gn_ref = refs[:11]
    pos = 11
    if has_s0:
        s0_ref = refs[pos]
        pos += 1
    o_ref, s_out_ref, s_ref = refs[pos:pos + 3]
    c = pl.program_id(1)
    nc = pl.num_programs(1)

    @pl.when(c == 0)
    def _():
        if has_s0:
            s_ref[...] = s0_ref[0]
        else:
            s_ref[...] = jnp.zeros_like(s_ref)

    cos = cos_ref[...]
    sin = sin_ref[...]

    def pad(t):
        if T == Cp:
            return t
        return jnp.concatenate([t, jnp.zeros((Cp - T, t.shape[1]), t.dtype)], axis=0)

    for h in range(RET_HEADS):
        sl = slice(h * RET_DK, (h + 1) * RET_DK)
        q = pad(q_ref[0, :, sl].astype(F32))
        k = pad(k_ref[0, :, sl].astype(F32))
        v = pad(v_ref[0, :, sl].astype(F32)).astype(BF16)
        q = q * cos + pltpu.roll(q, RET_DK // 2, 1) * sin
        k = (k * cos + pltpu.roll(k, RET_DK // 2, 1) * sin) * (RET_DK ** -0.5)
        qb = q.astype(BF16)
        att = _dot_nt(qb, k.astype(BF16)) * inner_ref[h]
        s_prev = s_ref[h]
        o = _dot(att.astype(BF16), v) + _dot(qb, s_prev.astype(BF16)) * qd_ref[h]
        s_ref[h] = s_prev * cd_ref[h] + _dot_tn((k * kd_ref[h]).astype(BF16), v)
        mu = jnp.mean(o, axis=-1, keepdims=True)
        d = o - mu
        var = jnp.mean(d * d, axis=-1, keepdims=True)
        y = d * lax.rsqrt(var + GN_EPS) * gn_ref[:, sl]
        gate = g_ref[0, :, sl].astype(F32)
        gate = gate * jax.nn.sigmoid(gate)
        o_ref[0, :, sl] = (gate * y[:T]).astype(o_ref.dtype)

    @pl.when(c == nc - 1)
    def _():
        s_out_ref[0] = s_ref[...]


def _retention_tables(C, Cp):
    H = RET_HEADS
    lg = jnp.log1p(-jnp.exp2(-5.0 - jnp.arange(H, dtype=F32)))
    i = jnp.arange(C, dtype=F32)
    diff = i[:, None] - i[None, :]
    inner = jnp.where((diff >= 0)[None], jnp.exp(jnp.maximum(diff, 0.0)[None] * lg[:, None, None]), 0.0)
    qd = jnp.exp((i[None, :] + 1.0) * lg[:, None])
    kd = jnp.exp((C - 1.0 - i)[None, :] * lg[:, None])
    cd = jnp.exp(C * lg)
    inner = jnp.pad(inner, ((0, 0), (0, Cp - C), (0, Cp - C)))
    qd = jnp.broadcast_to(jnp.pad(qd, ((0, 0), (0, Cp - C)))[:, :, None], (H, Cp, LANES))
    kd = jnp.broadcast_to(jnp.pad(kd, ((0, 0), (0, Cp - C)))[:, :, None], (H, Cp, LANES))
    cd = jnp.broadcast_to(cd[:, None, None], (H, 1, LANES))
    return inner, qd, kd, cd


def _rope_tables(pos, rows):
    half = RET_DK // 2
    inv = ROPE_BASE ** (-jnp.arange(half, dtype=F32) / half)
    ang = pos.astype(F32)[:, None] * inv[None, :]
    cos = jnp.cos(ang)
    sin = jnp.sin(ang)
    cos_full = jnp.concatenate([cos, cos], axis=-1)
    sin_signed = jnp.concatenate([-sin, sin], axis=-1)
    padn = rows - pos.shape[0]
    return jnp.pad(cos_full, ((0, padn), (0, 0))), jnp.pad(sin_signed, ((0, padn), (0, 0)))


def _retention(z, pos, gn, s0, C, out_dtype):
    B, L, _ = z.shape
    n = L // C
    Cp = max(C, LANES)
    inner, qd, kd, cd = _retention_tables(C, Cp)
    cos, sin = _rope_tables(pos, n * Cp)
    has_s0 = s0 is not None
    H = RET_HEADS

    def seg(k):
        return pl.BlockSpec((1, C, HEAD_COLS), lambda b, c, k=k: (b, c, k))

    const3 = lambda b, c: (0, 0, 0)
    in_specs = [seg(0), seg(1), seg(2), seg(3),
                pl.BlockSpec((Cp, LANES), lambda b, c: (c, 0)),
                pl.BlockSpec((Cp, LANES), lambda b, c: (c, 0)),
                pl.BlockSpec((H, Cp, Cp), const3),
                pl.BlockSpec((H, Cp, LANES), const3),
                pl.BlockSpec((H, Cp, LANES), const3),
                pl.BlockSpec((H, 1, LANES), const3),
                pl.BlockSpec((1, HEAD_COLS), lambda b, c: (0, 0))]
    args = [z, z, z, z, cos, sin, inner, qd, kd, cd, gn.reshape(1, HEAD_COLS)]
    if has_s0:
        in_specs.append(pl.BlockSpec((1, H, RET_DK, RET_DV), lambda b, c: (b, 0, 0, 0)))
        args.append(s0)
    return pl.pallas_call(
        functools.partial(_retention_kernel, T=C, Cp=Cp, has_s0=has_s0),
        grid=(B, n),
        in_specs=in_specs,
        out_specs=[pl.BlockSpec((1, C, HEAD_COLS), lambda b, c: (b, c, 0)),
                   pl.BlockSpec((1, H, RET_DK, RET_DV), lambda b, c: (b, 0, 0, 0))],
        out_shape=[jax.ShapeDtypeStruct((B, L, HEAD_COLS), out_dtype),
                   jax.ShapeDtypeStruct((B, H, RET_DK, RET_DV), F32)],
        scratch_shapes=[pltpu.VMEM((H, RET_DK, RET_DV), F32)],
        compiler_params=_params(("parallel", "arbitrary")),
        name="retention",
    )(*args)


def _split3(x):
    hi = x.astype(BF16).astype(F32)
    r = x - hi
    mid = r.astype(BF16).astype(F32)
    lo = (r - mid).astype(BF16).astype(F32)
    return hi, mid, lo


def _lane_scan(x, stride, strict_suffix):
    n = x.shape[0]
    j = lax.broadcasted_iota(jnp.int32, (LANES, LANES), 0)
    s = lax.broadcasted_iota(jnp.int32, (LANES, LANES), 1)
    same = ((j - s) & (stride - 1)) == 0
    tri = jnp.logical_and(same, (j > s) if strict_suffix else (j <= s))
    rhs = jnp.concatenate([jnp.where(tri, 1.0, 0.0), jnp.where(same, 1.0, 0.0)], axis=1).astype(F32)
    hi, mid, lo = _split3(x)
    y = _dot(jnp.concatenate([hi, mid, lo], axis=0), rhs)
    y = y[0:n] + y[n:2 * n] + y[2 * n:3 * n]
    return y[:, :LANES], y[:, LANES:]


def _cumsum_kernel(x_ref, o_ref, *, n_chunks):
    carry = jnp.zeros((SUBLANES, LANES), F32)
    for c in range(n_chunks):
        sl = slice(c * LANES, (c + 1) * LANES)
        scan, tot = _lane_scan(x_ref[0, :, sl], 1, strict_suffix=False)
        o_ref[0, :, sl] = scan + carry
        carry = carry + tot


def _cumsum_lanes(x):
    B, H, N = x.shape
    return pl.pallas_call(
        functools.partial(_cumsum_kernel, n_chunks=N // LANES),
        grid=(B,),
        in_specs=[pl.BlockSpec((1, H, N), lambda b: (b, 0, 0))],
        out_specs=pl.BlockSpec((1, H, N), lambda b: (b, 0, 0)),
        out_shape=jax.ShapeDtypeStruct((B, H, N), F32),
        compiler_params=_params(("parallel",)),
        name="logf_cumsum",
    )(x)


def _fox_prompt_kernel(q_ref, k_ref, v_ref, cq_ref, ck_ref, o_ref, m_ref, l_ref, acc_ref, *, tq):
    qi = pl.program_id(1)
    ki = pl.program_id(2)
    kscale = (FOX_HD ** -0.5) * LOG2E

    @pl.when(ki == 0)
    def _():
        m_ref[...] = jnp.full_like(m_ref, -jnp.inf)
        l_ref[...] = jnp.zeros_like(l_ref)
        acc_ref[...] = jnp.zeros_like(acc_ref)

    def scores(h):
        sl = slice(h * FOX_HD, (h + 1) * FOX_HD)
        k = (k_ref[0, :, sl] * kscale).astype(BF16)
        return _dot_nt(q_ref[0, :, sl], k)

    def step(diagonal):
        if diagonal:
            row = lax.broadcasted_iota(jnp.int32, (tq, tq), 0)
            col = lax.broadcasted_iota(jnp.int32, (tq, tq), 1)
            causal = row >= col
        ck = ck_ref[0] * LOG2E
        s_next = scores(0)
        for h in range(FOX_HEADS):
            sl = slice(h * FOX_HD, (h + 1) * FOX_HD)
            s = s_next
            if h + 1 < FOX_HEADS:
                s_next = scores(h + 1)
            t = s - ck[h:h + 1, :]
            if diagonal:
                t = jnp.where(causal, t, -jnp.inf)
            cq = cq_ref[0, :, h:h + 1] * LOG2E
            m_prev = m_ref[h]
            m_new = jnp.maximum(m_prev, jnp.max(t, axis=1, keepdims=True) + cq)
            corr = jnp.exp2(m_prev - m_new)
            p = jnp.exp2(t - (m_new - cq))
            psum = p[:, :LANES]
            for c in range(1, tq // LANES):
                psum = psum + p[:, c * LANES:(c + 1) * LANES]
            l_ref[h] = l_ref[h] * corr + psum
            v = v_ref[0, :, sl].astype(BF16)
            acc_ref[:, sl] = acc_ref[:, sl] * corr + _dot(p.astype(BF16), v)
            m_ref[h] = m_new

    @pl.when(ki < qi)
    def _():
        step(False)

    @pl.when(ki == qi)
    def _():
        step(True)
        for h in range(FOX_HEADS):
            sl = slice(h * FOX_HD, (h + 1) * FOX_HD)
            l = jnp.sum(l_ref[h], axis=1, keepdims=True)
            o_ref[0, :, sl] = (acc_ref[:, sl] / l).astype(o_ref.dtype)


def _fox_prompt(zb, fk, fv, ct_col, ct_row, tq):
    B, L, _ = fk.shape
    n = L // tq
    kv_map = lambda b, qi, ki: (b, jnp.minimum(ki, qi), 0)
    return pl.pallas_call(
        functools.partial(_fox_prompt_kernel, tq=tq),
        grid=(B, n, n),
        in_specs=[pl.BlockSpec((1, tq, HEAD_COLS), lambda b, qi, ki: (b, qi, COL_FQ)),
                  pl.BlockSpec((1, tq, HEAD_COLS), kv_map),
                  pl.BlockSpec((1, tq, HEAD_COLS), kv_map),
                  pl.BlockSpec((1, tq, FOX_HEADS), lambda b, qi, ki: (b, qi, 0)),
                  pl.BlockSpec((1, FOX_HEADS, tq), lambda b, qi, ki: (b, 0, jnp.minimum(ki, qi)))],
        out_specs=pl.BlockSpec((1, tq, HEAD_COLS), lambda b, qi, ki: (b, qi, 0)),
        out_shape=jax.ShapeDtypeStruct((B, L, HEAD_COLS), BF16),
        scratch_shapes=[pltpu.VMEM((FOX_HEADS, tq, 1), F32),
                        pltpu.VMEM((FOX_HEADS, tq, LANES), F32),
                        pltpu.VMEM((tq, HEAD_COLS), F32)],
        compiler_params=_params(("parallel", "parallel", "arbitrary")),
        name="fox_prompt",
    )(zb, fk, fv, ct_col, ct_row)


def _mem_attend_kernel(q_ref, k_ref, v_ref, o_ref):
    scale = MEM_HD ** -0.5
    for h in range(MEM_HEADS):
        sl = slice(h * MEM_HD, (h + 1) * MEM_HD)
        q = q_ref[0, :, sl].astype(BF16)
        k = (k_ref[0, :, sl] * scale).astype(BF16)
        v = v_ref[0, :, sl].astype(BF16)
        s = _dot_nt(q, k)
        p = jnp.exp(s - jnp.max(s, axis=1, keepdims=True))
        o = _dot(p.astype(BF16), v) / jnp.sum(p, axis=1, keepdims=True)
        o_ref[0, :, sl] = o.astype(o_ref.dtype)


def _mem_attend(zb, mk, mv, tq, out_dtype):
    B, L, _ = zb.shape
    M = mk.shape[1]
    return pl.pallas_call(
        _mem_attend_kernel,
        grid=(B, L // tq),
        in_specs=[pl.BlockSpec((1, tq, HEAD_COLS), lambda b, i: (b, i, COL_MQ)),
                  pl.BlockSpec((1, M, HEAD_COLS), lambda b, i: (b, 0, 0)),
                  pl.BlockSpec((1, M, HEAD_COLS), lambda b, i: (b, 0, 0))],
        out_specs=pl.BlockSpec((1, tq, HEAD_COLS), lambda b, i: (b, i, 0)),
        out_shape=jax.ShapeDtypeStruct((B, L, HEAD_COLS), out_dtype),
        compiler_params=_params(("parallel", "parallel")),
        name="mem_attend",
    )(zb, mk, mv)


def _merge_kernel(x_ref, oret_ref, ofox_ref, omem_ref, gate_ref, wr_ref, wf_ref, wm_ref, wo_ref, o_ref, *, D):
    def branch(k, o_b, w_ref):
        g = jax.nn.sigmoid(gate_ref[:, k * D:(k + 1) * D].astype(F32))
        return g * _dot(o_b[...].astype(BF16), w_ref[...])

    merged = branch(0, oret_ref, wr_ref) + branch(1, ofox_ref, wf_ref) + branch(2, omem_ref, wm_ref)
    o_ref[...] = x_ref[...] + _dot(merged.astype(BF16), wo_ref[...])


def _merge(x2d, o_ret, o_fox, o_mem, zb, w_ret, w_fox, w_mem, w_o, tm):
    T, D = x2d.shape
    gate_blk = COL_GATES * HEAD_COLS // (N_BRANCH * D)
    assert gate_blk * N_BRANCH * D == COL_GATES * HEAD_COLS
    row = lambda i: (i, 0)
    const = lambda i: (0, 0)
    resident = functools.partial(pl.BlockSpec, index_map=const, pipeline_mode=pl.Buffered(1))
    return pl.pallas_call(
        functools.partial(_merge_kernel, D=D),
        grid=(T // tm,),
        in_specs=[pl.BlockSpec((tm, D), row),
                  pl.BlockSpec((tm, HEAD_COLS), row),
                  pl.BlockSpec((tm, HEAD_COLS), row),
                  pl.BlockSpec((tm, HEAD_COLS), row),
                  pl.BlockSpec((tm, N_BRANCH * D), lambda i: (i, gate_blk)),
                  resident((HEAD_COLS, D)),
                  resident((HEAD_COLS, D)),
                  resident((HEAD_COLS, D)),
                  resident((D, D))],
        out_specs=pl.BlockSpec((tm, D), row),
        out_shape=jax.ShapeDtypeStruct((T, D), F32),
        compiler_params=_params(("parallel",)),
        name="merge",
    )(x2d, o_ret, o_fox, o_mem, zb, w_ret, w_fox, w_mem, w_o)


def _mlp_kernel(x_ref, g_ref, wu_ref, wd_ref, gf_ref, o_ref, hn_ref, *, final):
    f = pl.program_id(1)
    nf = pl.num_programs(1)

    @pl.when(f == 0)
    def _():
        x = x_ref[...]
        hn_ref[...] = _rms_normed(x, g_ref[...]).astype(BF16)
        o_ref[...] = x

    u = jnp.maximum(_dot(hn_ref[...], wu_ref[...]), 0.0)
    o_ref[...] += _dot((u * u).astype(BF16), wd_ref[...])

    if final:
        @pl.when(f == nf - 1)
        def _():
            o_ref[...] = _rms_normed(o_ref[...], gf_ref[...])


def _mlp(x2d, g, w_up, w_down, g_final, tm, tf, final):
    T, D = x2d.shape
    Fd = w_up.shape[1]
    return pl.pallas_call(
        functools.partial(_mlp_kernel, final=final),
        grid=(T // tm, Fd // tf),
        in_specs=[pl.BlockSpec((tm, D), lambda i, f: (i, 0)),
                  pl.BlockSpec((1, D), lambda i, f: (0, 0)),
                  pl.BlockSpec((D, tf), lambda i, f: (0, f)),
                  pl.BlockSpec((tf, D), lambda i, f: (f, 0)),
                  pl.BlockSpec((1, D), lambda i, f: (0, 0))],
        out_specs=pl.BlockSpec((tm, D), lambda i, f: (i, 0)),
        out_shape=jax.ShapeDtypeStruct((T, D), F32),
        scratch_shapes=[pltpu.VMEM((tm, D), BF16)],
        compiler_params=_params(("parallel", "arbitrary")),
        name="mlp",
    )(x2d, g.reshape(1, D), w_up, w_down, g_final.reshape(1, D))


def _fox_new_tokens(q, kn, vn, lfn, qall_ref, ptcol_ref, m_ref, l_ref, acc_ref, carry_ref):
    H = FOX_HEADS
    R = q.shape[0]
    row = lax.broadcasted_iota(jnp.int32, (R, LANES), 0)
    lane = lax.broadcasted_iota(jnp.int32, (R, LANES), 1)
    qall = (q * (FOX_HD ** -0.5)).astype(BF16)
    qall_ref[...] = qall
    ptrow = _lane_scan(jnp.broadcast_to(lfn, (SUBLANES, LANES)), H, strict_suffix=False)[0][0:1]
    ptcol = jnp.sum(jnp.where(lane == row, ptrow, 0.0), axis=1, keepdims=True)
    ptcol_ref[...] = ptcol
    zpad = jnp.zeros((LANES - R, FOX_HD), F32)
    kn = jnp.concatenate([kn, zpad], axis=0).astype(BF16)
    vn = jnp.concatenate([vn, zpad], axis=0).astype(BF16)
    s = _dot_nt(qall, kn) + (ptcol - ptrow)
    causal = jnp.logical_and(((lane - row) & (H - 1)) == 0, (lane // H) <= (row // H))
    s = jnp.where(causal, s, -jnp.inf)
    m = jnp.max(s, axis=1, keepdims=True)
    p = jnp.exp(s - m)
    m_ref[...] = m
    l_ref[...] = jnp.sum(p, axis=1, keepdims=True)
    acc_ref[...] = _dot(p.astype(BF16), vn)
    carry_ref[...] = jnp.zeros_like(carry_ref)


def _fox_pages(k_pages, v_pages, lf_pages, qall_ref, ptcol_ref, m_ref, l_ref, acc_ref, carry_ref):
    H = FOX_HEADS
    G = len(k_pages)
    R = qall_ref.shape[0]
    row = lax.broadcasted_iota(jnp.int32, (R, LANES), 0)
    lane = lax.broadcasted_iota(jnp.int32, (R, LANES), 1)
    same_head = ((lane - row) & (H - 1)) == 0
    suf, rowtot = _lane_scan(jnp.concatenate(lf_pages, axis=0), H, strict_suffix=True)
    sub = lax.broadcasted_iota(jnp.int32, (SUBLANES, LANES), 0)
    qall = qall_ref[...]
    ptcol = ptcol_ref[...]
    carry = carry_ref[0:1, :]
    tiles = []
    tmax = None
    for i in range(G):
        pre = rowtot[i * SUBLANES:(i + 1) * SUBLANES]
        for sh in (1, 2, 4):
            pre = pre + jnp.where(sub >= sh, pltpu.roll(pre, sh, 0), 0.0)
        page_tot = pre[SUBLANES - 1:SUBLANES]
        later = suf[i * SUBLANES:(i + 1) * SUBLANES] + (page_tot - pre) + carry
        carry = carry + page_tot
        s2 = _dot_nt(qall, k_pages[i][...].astype(BF16))
        for r in range(SUBLANES):
            t = jnp.where(same_head, s2[:, r * LANES:(r + 1) * LANES] + later[r:r + 1, :], -jnp.inf)
            tiles.append(t)
            tmax = t if tmax is None else jnp.maximum(tmax, t)
    carry_ref[...] = jnp.broadcast_to(carry, carry_ref.shape)
    m_prev = m_ref[...]
    m_new = jnp.maximum(m_prev, jnp.max(tmax, axis=1, keepdims=True) + ptcol)
    shift = m_new - ptcol
    corr = jnp.exp(m_prev - m_new)
    psum = jnp.zeros((R, LANES), F32)
    pv = jnp.zeros((R, FOX_HD), F32)
    for i in range(G):
        p = [jnp.exp(t - shift) for t in tiles[i * SUBLANES:(i + 1) * SUBLANES]]
        for pt in p:
            psum = psum + pt
        pv = pv + _dot(jnp.concatenate(p, axis=1).astype(BF16), v_pages[i][...].astype(BF16))
    l_ref[...] = l_ref[...] * corr + jnp.sum(psum, axis=1, keepdims=True)
    acc_ref[...] = acc_ref[...] * corr + pv
    m_ref[...] = m_new


def _page_copies(pt_ref, kc_ref, vc_ref, kbuf, vbuf, sem, u, slot, G, NG, layer):
    n_pages = NG * G
    bb = lax.div(u, NG)
    gg = lax.rem(u, NG)
    out = []
    for i in range(G):
        page = pt_ref[bb, n_pages - 1 - (gg * G + i)]
        out.append(pltpu.make_async_copy(kc_ref.at[layer, page], kbuf.at[slot, i], sem.at[0, slot]))
        out.append(pltpu.make_async_copy(vc_ref.at[layer, page], vbuf.at[slot, i], sem.at[1, slot]))
    return out


def _fox_sample_kernel(pt_ref, q_ref, kn_ref, vn_ref, lfn_ref, kc_ref, vc_ref, *refs, G, NG, layer):
    lf_refs = refs[:G]
    o_ref, kbuf, vbuf, sem, qall_ref, ptcol_ref, m_ref, l_ref, acc_ref, carry_ref = refs[G:]
    g = pl.program_id(1)
    n_slots = kbuf.shape[0]
    step = pl.program_id(0) * NG + g
    n_steps = pl.num_programs(0) * NG
    page_copies = functools.partial(_page_copies, pt_ref, kc_ref, vc_ref, kbuf, vbuf, sem,
                                    G=G, NG=NG, layer=layer)

    @pl.when(step == 0)
    def _():
        for j in range(n_slots - 1):
            for c in page_copies(j, j):
                c.start()

    ahead = step + (n_slots - 1)

    @pl.when(ahead < n_steps)
    def _():
        for c in page_copies(ahead, lax.rem(ahead, n_slots)):
            c.start()

    slot = lax.rem(step, n_slots)
    for c in page_copies(step, slot):
        c.wait()
    state = (qall_ref, ptcol_ref, m_ref, l_ref, acc_ref, carry_ref)

    @pl.when(g == 0)
    def _():
        _fox_new_tokens(q_ref[0], kn_ref[0], vn_ref[0], lfn_ref[0], *state)

    _fox_pages([kbuf.at[slot, i] for i in range(G)], [vbuf.at[slot, i] for i in range(G)],
               [lf_refs[i][0, 0] for i in range(G)], *state)

    @pl.when(g == NG - 1)
    def _():
        o_ref[0] = acc_ref[...] / l_ref[...]


def _fox_sample(fq, fk, fv, lf_new, cache_k, cache_v, cache_lf, page_table, layer, G):
    B, R, _ = fq.shape
    n_pages = page_table.shape[1]
    rows = cache_k.shape[2]
    assert rows == SUBLANES * LANES and n_pages % G == 0 and R <= LANES
    NG = n_pages // G
    tok = pl.BlockSpec((1, R, FOX_HD), lambda b, g, pt: (b, 0, 0))

    def lf_spec(i):
        return pl.BlockSpec((1, 1, SUBLANES, LANES),
                            lambda b, g, pt, i=i: (layer, pt[b, n_pages - 1 - (g * G + i)], 0, 0))

    assert B * NG >= PAGE_RING_SLOTS - 1
    in_specs = ([tok, tok, tok, pl.BlockSpec((1, 1, LANES), lambda b, g, pt: (b, 0, 0)),
                 pl.BlockSpec(memory_space=pl.ANY), pl.BlockSpec(memory_space=pl.ANY)]
                + [lf_spec(i) for i in range(G)])
    grid_spec = pltpu.PrefetchScalarGridSpec(
        num_scalar_prefetch=1,
        grid=(B, NG),
        in_specs=in_specs,
        out_specs=pl.BlockSpec((1, R, FOX_HD), lambda b, g, pt: (b, 0, 0)),
        scratch_shapes=[pltpu.VMEM((PAGE_RING_SLOTS, G, rows, FOX_HD), F32),
                        pltpu.VMEM((PAGE_RING_SLOTS, G, rows, FOX_HD), F32),
                        pltpu.SemaphoreType.DMA((2, PAGE_RING_SLOTS)),
                        pltpu.VMEM((R, FOX_HD), BF16),
                        pltpu.VMEM((R, 1), F32),
                        pltpu.VMEM((R, 1), F32),
                        pltpu.VMEM((R, 1), F32),
                        pltpu.VMEM((R, FOX_HD), F32),
                        pltpu.VMEM((SUBLANES, LANES), F32)])
    return pl.pallas_call(
        functools.partial(_fox_sample_kernel, G=G, NG=NG, layer=layer),
        grid_spec=grid_spec,
        out_shape=jax.ShapeDtypeStruct((B, R, FOX_HD), F32),
        compiler_params=_params(("arbitrary", "arbitrary")),
        name="fox_sample",
    )(page_table, fq, fk, fv, lf_new, cache_k, cache_v, *([cache_lf] * G))


def _mlp_fox_kernel(pt_ref, x_hbm, g_ref, wu_ref, wd_ref, gf_ref, q_ref, kn_ref, vn_ref, lfn_ref,
                    kc_ref, vc_ref, *refs, G, NG, units, layer, final):
    n_lf = units * G
    lf_refs = refs[:n_lf]
    (o_ref, fo_ref, hn_ref, xsem, kbuf, vbuf, sem,
     qall_ref, ptcol_ref, m_ref, l_ref, acc_ref, carry_ref) = refs[n_lf:]
    i = pl.program_id(0)
    f = pl.program_id(1)
    nf = pl.num_programs(1)
    step = i * nf + f
    n_units = pl.num_programs(0) * nf * units
    n_slots = kbuf.shape[0]
    tm = o_ref.shape[0]
    part = tm // units
    state = (qall_ref, ptcol_ref, m_ref, l_ref, acc_ref, carry_ref)
    page_copies = functools.partial(_page_copies, pt_ref, kc_ref, vc_ref, kbuf, vbuf, sem,
                                    G=G, NG=NG, layer=layer)

    @pl.when(f == 0)
    def _():
        cp = pltpu.make_async_copy(x_hbm.at[pl.ds(i * tm, tm)], o_ref, xsem.at[0])
        cp.start()
        cp.wait()
        hn_ref[...] = _rms_normed(o_ref[...], g_ref[...]).astype(BF16)

    @pl.when(step == 0)
    def _():
        for u0 in range(n_slots):
            for c in page_copies(u0, u0):
                c.start()

    for j in range(units):
        u = step * units + j
        slot = lax.rem(u, n_slots)
        b = lax.div(u, NG)
        if j == 0:
            @pl.when(lax.rem(u, NG) == 0)
            def _(b=b):
                _fox_new_tokens(q_ref[b], kn_ref[b], vn_ref[b], lfn_ref[b], *state)

        for c in page_copies(u, slot):
            c.wait()
        rows = slice(j * part, (j + 1) * part)
        h = jnp.maximum(_dot(hn_ref[rows], wu_ref[...]), 0.0)
        o_ref[rows] += _dot((h * h).astype(BF16), wd_ref[...])
        _fox_pages([kbuf.at[slot, p] for p in range(G)], [vbuf.at[slot, p] for p in range(G)],
                   [lf_refs[j * G + p][0, 0] for p in range(G)], *state)
        nxt = u + n_slots

        @pl.when(nxt < n_units)
        def _(nxt=nxt, slot=slot):
            for c in page_copies(nxt, slot):
                c.start()

        if j == units - 1:
            @pl.when(lax.rem(u, NG) == NG - 1)
            def _(b=b):
                fo_ref[b] = acc_ref[...] / l_ref[...]

    if final:
        @pl.when(f == nf - 1)
        def _():
            o_ref[...] = _rms_normed(o_ref[...], gf_ref[...])


def _mlp_fox(x2d, g, w_up, w_down, g_final, final, tm, tf, fq, fk, fv, lf_new, cache_k, cache_v, cache_lf,
             page_table, layer, G, units):
    T, D = x2d.shape
    Fd = w_up.shape[1]
    B, R, _ = fq.shape
    n_pages = page_table.shape[1]
    rows = cache_k.shape[2]
    NG = n_pages // G
    ni, nf = T // tm, Fd // tf
    assert rows == SUBLANES * LANES and NG * G == n_pages and R <= LANES
    assert NG % units == 0 and ni * nf * units == B * NG and tm % units == 0
    assert B * NG >= MLP_FOX_RING_SLOTS

    def lf_spec(j, p):
        def index_map(i, f, pt):
            u = (i * nf + f) * units + j
            return (layer, pt[lax.div(u, NG), n_pages - 1 - (lax.rem(u, NG) * G + p)], 0, 0)
        return pl.BlockSpec((1, 1, SUBLANES, LANES), index_map)

    whole = lambda shape: pl.BlockSpec(shape, lambda i, f, pt: (0,) * len(shape), pipeline_mode=pl.Buffered(1))
    any_space = pl.BlockSpec(memory_space=pl.ANY)
    in_specs = ([any_space,
                 pl.BlockSpec((1, D), lambda i, f, pt: (0, 0)),
                 pl.BlockSpec((D, tf), lambda i, f, pt: (0, f)),
                 pl.BlockSpec((tf, D), lambda i, f, pt: (f, 0)),
                 pl.BlockSpec((1, D), lambda i, f, pt: (0, 0)),
                 whole((B, R, FOX_HD)), whole((B, R, FOX_HD)), whole((B, R, FOX_HD)), whole((B, 1, LANES)),
                 any_space, any_space]
                + [lf_spec(j, p) for j in range(units) for p in range(G)])
    grid_spec = pltpu.PrefetchScalarGridSpec(
        num_scalar_prefetch=1,
        grid=(ni, nf),
        in_specs=in_specs,
        out_specs=[pl.BlockSpec((tm, D), lambda i, f, pt: (i, 0)),
                   pl.BlockSpec((B, R, FOX_HD), lambda i, f, pt: (0, 0, 0))],
        scratch_shapes=[pltpu.VMEM((tm, D), BF16),
                        pltpu.SemaphoreType.DMA((1,)),
                        pltpu.VMEM((MLP_FOX_RING_SLOTS, G, rows, FOX_HD), F32),
                        pltpu.VMEM((MLP_FOX_RING_SLOTS, G, rows, FOX_HD), F32),
                        pltpu.SemaphoreType.DMA((2, MLP_FOX_RING_SLOTS)),
                        pltpu.VMEM((R, FOX_HD), BF16),
                        pltpu.VMEM((R, 1), F32),
                        pltpu.VMEM((R, 1), F32),
                        pltpu.VMEM((R, 1), F32),
                        pltpu.VMEM((R, FOX_HD), F32),
                        pltpu.VMEM((SUBLANES, LANES), F32)])
    return pl.pallas_call(
        functools.partial(_mlp_fox_kernel, G=G, NG=NG, units=units, layer=layer, final=final),
        grid_spec=grid_spec,
        out_shape=[jax.ShapeDtypeStruct((T, D), F32), jax.ShapeDtypeStruct((B, R, FOX_HD), F32)],
        compiler_params=_params(("arbitrary", "arbitrary")),
        name="mlp_fox",
    )(page_table, x2d, g.reshape(1, D), w_up, w_down, g_final.reshape(1, D), fq, fk, fv, lf_new,
      cache_k, cache_v, *([cache_lf] * (units * G)))


def _pick(n, prefs):
    for p in prefs:
        if n % p == 0:
            return p
    return n


def _shared_tiling(mlp_steps, n_rows, n_pages):
    for G in (8, 4, 2, 1):
        for units in (2, 1, 4):
            if n_pages % G == 0 and (n_pages // G) % units == 0 and n_rows * (n_pages // G) == mlp_steps * units:
                return G, units
    return None


def _split_w_in(w_in, b_ff_l, layer):
    c = HEAD_COLS
    w_all = _regroup_cast(jnp.swapaxes(w_in, 1, 2), layer, _pick(w_in.shape[1], (512, 256)))
    w_ff = jnp.pad(w_in[layer, :, 7 * c:7 * c + FOX_HEADS], ((0, 0), (0, LANES - FOX_HEADS))).astype(BF16)
    b_ff = jnp.pad(b_ff_l.astype(F32), (0, LANES - FOX_HEADS)).reshape(1, LANES)
    return w_all, w_ff, b_ff


def kernel(x_prompt, x_sample, mem_prompt, state_ret, cache_fox_k, cache_fox_v, cache_fox_logf,
           cache_mem_k, cache_mem_v, page_table, norm_mix, w_in, b_fox_f, gn_ret, norm_mem, w_mem_kv,
           w_br_ret, w_br_fox, w_br_mem, w_o, norm_mlp, w_up, w_down, norm_final):
    B, L, D = x_prompt.shape
    Bd, T, _ = x_sample.shape
    depth = w_in.shape[0]
    n_pool, ps = cache_fox_k.shape[1], cache_fox_k.shape[2]
    n_pages = page_table.shape[1]
    M = mem_prompt.shape[1]
    H = FOX_HEADS
    past_len = n_pages * ps
    pos_p = jnp.arange(L, dtype=jnp.int32)
    pos_s = past_len + jnp.arange(T, dtype=jnp.int32)

    tn = _pick(D, (1024, 256))
    tm_p = _pick(B * L, (1024, 512, 256))
    tm_kv = _pick(B * L, (512, 256))
    C_ret = _pick(L, (256, 128))
    tq_fox = _pick(L, (512, 256, 128))
    tq_mem = _pick(L, (512, 256, 128))
    tm_merge = _pick(B * L, (256,))
    tm_mlp = _pick(B * L, (1024, 512, 256))
    tf_mlp = _pick(w_up.shape[2], (512,))
    G = _pick(n_pages, (8, 4, 2))

    cache_k = cache_fox_k.reshape(depth, n_pool, ps * H, FOX_HD)
    cache_v = cache_fox_v.reshape(depth, n_pool, ps * H, FOX_HD)
    cache_lf = cache_fox_logf.reshape(depth, n_pool, ps * H // LANES, LANES)

    xp = x_prompt.reshape(B * L, D)
    xs = x_sample.reshape(Bd * T, D)
    outs = {k: [] for k in ("sp", "ss", "kp", "vp", "fp", "ks", "vs", "fs", "mk", "mv")}
    for l in range(depth):
        last = l == depth - 1
        w_all, w_ff, b_ff = _split_w_in(w_in, b_fox_f[l], l)
        kv_cols = (COL_KV * HEAD_COLS, 2 * HEAD_COLS)
        w_ret = w_br_ret[l].astype(BF16)
        w_fox = w_br_fox[l].astype(BF16)
        w_mem = w_br_mem[l].astype(BF16)
        w_o_l = w_o[l].astype(BF16)
        w_up_l = w_up[l].astype(BF16)
        w_down_l = w_down[l].astype(BF16)

        zb = _norm_proj(xp, norm_mix[l], w_all, *kv_cols, tm_p, tn, BF16)
        fk, fv, logf = _norm_kv(xp, norm_mix[l], w_all, COL_KV, tm_kv, w_ff, b_ff)
        zb3 = zb.reshape(B, L, zb.shape[1])
        o_ret, s_new = _retention(zb3, pos_p, gn_ret[l], None, C_ret, BF16)
        logf3 = logf[:, :H].reshape(B, L, H)
        ct_row = _cumsum_lanes(jnp.swapaxes(logf3, 1, 2))
        o_fox = _fox_prompt(zb3, fk.reshape(B, L, HEAD_COLS), fv.reshape(B, L, HEAD_COLS),
                            jnp.swapaxes(ct_row, 1, 2), ct_row, tq_fox)
        mk, mv = _norm_kv(mem_prompt.reshape(B * M, D), norm_mem[l], w_mem_kv[l].astype(BF16), 0,
                          _pick(B * M, (512, 256)))
        o_mem = _mem_attend(zb3, mk.reshape(B, M, HEAD_COLS), mv.reshape(B, M, HEAD_COLS), tq_mem, BF16)
        xp = _merge(xp, o_ret.reshape(B * L, HEAD_COLS), o_fox.reshape(B * L, HEAD_COLS),
                    o_mem.reshape(B * L, HEAD_COLS), zb, w_ret, w_fox, w_mem, w_o_l, tm_merge)
        zs = _norm_proj(xs, norm_mix[l], w_all, *kv_cols, Bd * T, tn, F32)
        fk_s, fv_s, logf_s = _norm_kv(xs, norm_mix[l], w_all, COL_KV, Bd * T, w_ff, b_ff)
        logf3_s = logf_s[:, :H].reshape(Bd, T, H)
        fox_in = (zs[:, COL_FQ * HEAD_COLS:(COL_FQ + 1) * HEAD_COLS].reshape(Bd, T * H, FOX_HD),
                  fk_s.reshape(Bd, T * H, FOX_HD), fv_s.reshape(Bd, T * H, FOX_HD),
                  jnp.pad(logf3_s.reshape(Bd, 1, T * H), ((0, 0), (0, 0), (0, LANES - T * H))),
                  cache_k, cache_v, cache_lf, page_table, l)
        shared = _shared_tiling(B * L // tm_mlp * (w_up.shape[2] // tf_mlp), Bd, n_pages)
        if shared is None:
            xp = _mlp(xp, norm_mlp[l], w_up_l, w_down_l, norm_final, tm_mlp, tf_mlp, last)
            o_fox_s = _fox_sample(*fox_in, G)
        else:
            xp, o_fox_s = _mlp_fox(xp, norm_mlp[l], w_up_l, w_down_l, norm_final, last, tm_mlp, tf_mlp,
                                   *fox_in, *shared)
        outs["sp"].append(s_new)
        outs["kp"].append(fk.reshape(B, L, H, FOX_HD))
        outs["vp"].append(fv.reshape(B, L, H, FOX_HD))
        outs["fp"].append(logf3)
        outs["mk"].append(mk.reshape(B, M, MEM_HEADS, MEM_HD))
        outs["mv"].append(mv.reshape(B, M, MEM_HEADS, MEM_HD))

        zs3 = zs.reshape(Bd, T, zs.shape[1])
        o_ret, s_new = _retention(zs3, pos_s, gn_ret[l], state_ret[l], T, F32)
        o_mem = _mem_attend(zs3, cache_mem_k[l].reshape(Bd, M, HEAD_COLS),
                            cache_mem_v[l].reshape(Bd, M, HEAD_COLS), T, F32)
        xs = _merge(xs, o_ret.reshape(Bd * T, HEAD_COLS), o_fox_s.reshape(Bd * T, HEAD_COLS),
                    o_mem.reshape(Bd * T, HEAD_COLS), zs, w_ret, w_fox, w_mem, w_o_l, Bd * T)
        xs = _mlp(xs, norm_mlp[l], w_up_l, w_down_l, norm_final, Bd * T, tf_mlp, last)
        outs["ss"].append(s_new)
        outs["ks"].append(fk_s.reshape(Bd, T, H, FOX_HD))
        outs["vs"].append(fv_s.reshape(Bd, T, H, FOX_HD))
        outs["fs"].append(logf3_s)

    st = lambda k: jnp.stack(outs[k])
    return (xp.reshape(B, L, D), xs.reshape(Bd, T, D), st("sp"), st("ss"), st("kp"), st("vp"), st("fp"),
            st("ks"), st("vs"), st("fs"), st("mk"), st("mv"))
```

```python
import functools
import math

import jax
import jax.numpy as jnp
from jax import lax
from jax.experimental import pallas as pl
from jax.experimental.pallas import tpu as pltpu

F32 = jnp.float32
BF16 = jnp.bfloat16

RET_HEADS = 8
RET_DK = 128
RET_DV = 128
FOX_HEADS = 8
FOX_HD = 128
MEM_HEADS = 4
MEM_HD = 256
N_BRANCH = 3
ROPE_BASE = 10000.0
RMS_EPS = 1e-6
GN_EPS = 1e-5

HEAD_COLS = 1024
LANES = 128
SUBLANES = 8
VMEM_LIMIT = 56 * 1024 * 1024
LOG2E = math.log2(math.e)
PAGE_RING_SLOTS = 3
MLP_FOX_RING_SLOTS = 2

COL_FQ = 4
COL_MQ = 5
COL_GATES = 6


def _dot(a, b):
    return jnp.dot(a, b, preferred_element_type=F32)


def _dot_nt(a, b):
    return lax.dot_general(a, b, (((1,), (1,)), ((), ())), preferred_element_type=F32)


def _dot_tn(a, b):
    return lax.dot_general(a, b, (((0,), (0,)), ((), ())), preferred_element_type=F32)


def _params(sem):
    return pltpu.CompilerParams(dimension_semantics=sem, vmem_limit_bytes=VMEM_LIMIT)


def _rms_normed(x, g):
    ms = jnp.mean(x * x, axis=-1, keepdims=True)
    return x * lax.rsqrt(ms + RMS_EPS) * g


def _log_sigmoid(x):
    return jnp.minimum(x, 0.0) - jnp.log1p(jnp.exp(-jnp.abs(x)))


def _norm_proj_kernel(x_ref, g_ref, w_ref, o_ref, hn_ref):
    @pl.when(pl.program_id(1) == 0)
    def _():
        hn_ref[...] = _rms_normed(x_ref[...], g_ref[...]).astype(BF16)

    o_ref[...] = _dot(hn_ref[...], w_ref[...]).astype(o_ref.dtype)


def _norm_proj(x2d, g, w, n_cols, tm, tn, out_dtype):
    T, D = x2d.shape
    assert n_cols % tn == 0
    return pl.pallas_call(
        _norm_proj_kernel,
        grid=(T // tm, n_cols // tn),
        in_specs=[pl.BlockSpec((tm, D), lambda i, j: (i, 0)),
                  pl.BlockSpec((1, D), lambda i, j: (0, 0)),
                  pl.BlockSpec((D, tn), lambda i, j: (0, j))],
        out_specs=pl.BlockSpec((tm, tn), lambda i, j: (i, j)),
        out_shape=jax.ShapeDtypeStruct((T, n_cols), out_dtype),
        scratch_shapes=[pltpu.VMEM((tm, D), BF16)],
        compiler_params=_params(("parallel", "arbitrary")),
        name="norm_proj",
    )(x2d, g.reshape(1, D), w)


def _regroup_cast_kernel(a_ref, b_ref, o_ref, *, n_lead, n_gate_blocks, shift):
    j = pl.program_id(1)
    shifted = (j >= n_lead) & (j < n_lead + 1 + n_gate_blocks)

    @pl.when(jnp.logical_not(shifted))
    def _():
        o_ref[...] = a_ref[...].T.astype(BF16)

    @pl.when(shifted)
    def _():
        x = jnp.concatenate([a_ref[shift:, :], b_ref[:shift, :]], axis=0)
        o_ref[...] = x.T.astype(BF16)


def _regroup_cast(w_t, layer, tr):
    _, N, D = w_t.shape
    n_lead = 5
    n_gate_blocks = pl.cdiv(N_BRANCH * D, HEAD_COLS)
    n_blocks = n_lead + 1 + n_gate_blocks + 2
    last = pl.cdiv(N, HEAD_COLS) - 1

    def src(j):
        return jnp.where(j < n_lead, j, jnp.where(j < n_lead + 1 + n_gate_blocks, j + 2, j - 1 - n_gate_blocks))

    def src_next(j):
        shifted = (j >= n_lead) & (j < n_lead + 1 + n_gate_blocks)
        return jnp.where(shifted, jnp.minimum(src(j) + 1, last), n_lead + 3)

    return pl.pallas_call(
        functools.partial(_regroup_cast_kernel, n_lead=n_lead, n_gate_blocks=n_gate_blocks, shift=FOX_HEADS),
        grid=(D // tr, n_blocks),
        in_specs=[pl.BlockSpec((None, HEAD_COLS, tr), lambda r, j: (layer, src(j), r)),
                  pl.BlockSpec((None, HEAD_COLS, tr), lambda r, j: (layer, src_next(j), r))],
        out_specs=pl.BlockSpec((tr, HEAD_COLS), lambda r, j: (r, j)),
        out_shape=jax.ShapeDtypeStruct((D, n_blocks * HEAD_COLS), BF16),
        compiler_params=_params(("parallel", "parallel")),
        name="regroup_cast",
    )(w_t, w_t)


def _norm_kv_kernel(*refs, has_ff):
    if has_ff:
        x_ref, g_ref, wk_ref, wv_ref, wff_ref, bff_ref, k_ref, v_ref, logf_ref = refs
    else:
        x_ref, g_ref, wk_ref, wv_ref, k_ref, v_ref = refs
    hn = _rms_normed(x_ref[...], g_ref[...]).astype(BF16)
    k_ref[...] = _dot(hn, wk_ref[...])
    v_ref[...] = _dot(hn, wv_ref[...])
    if has_ff:
        logf_ref[...] = _log_sigmoid(_dot(hn, wff_ref[...]) + bff_ref[...])


def _norm_kv(x2d, g, w, col_block, tm, w_ff=None, b_ff=None):
    T, D = x2d.shape
    has_ff = w_ff is not None
    row = lambda i: (i, 0)
    const = lambda i: (0, 0)
    in_specs = [pl.BlockSpec((tm, D), row), pl.BlockSpec((1, D), const),
                pl.BlockSpec((D, HEAD_COLS), lambda i: (0, col_block), pipeline_mode=pl.Buffered(1)),
                pl.BlockSpec((D, HEAD_COLS), lambda i: (0, col_block + 1), pipeline_mode=pl.Buffered(1))]
    args = [x2d, g.reshape(1, D), w, w]
    out_specs = [pl.BlockSpec((tm, HEAD_COLS), row), pl.BlockSpec((tm, HEAD_COLS), row)]
    out_shape = [jax.ShapeDtypeStruct((T, HEAD_COLS), F32), jax.ShapeDtypeStruct((T, HEAD_COLS), F32)]
    if has_ff:
        in_specs += [pl.BlockSpec((D, LANES), const), pl.BlockSpec((1, LANES), const)]
        args += [w_ff, b_ff]
        out_specs.append(pl.BlockSpec((tm, LANES), row))
        out_shape.append(jax.ShapeDtypeStruct((T, LANES), F32))
    return pl.pallas_call(
        functools.partial(_norm_kv_kernel, has_ff=has_ff),
        grid=(T // tm,),
        in_specs=in_specs,
        out_specs=out_specs,
        out_shape=out_shape,
        compiler_params=_params(("parallel",)),
        name="norm_kv",
    )(*args)


def _retention_kernel(*refs, T, Cp, has_s0):
    q_ref, k_ref, v_ref, g_ref, cos_ref, sin_ref, inner_ref, qd_ref, kd_ref, cd_ref, gn_ref = refs[:11]
    pos = 11
    if has_s0:
        s0_ref = refs[pos]
        pos += 1
    o_ref, s_out_ref, s_ref = refs[pos:pos + 3]
    c = pl.program_id(1)
    nc = pl.num_programs(1)

    @pl.when(c == 0)
    def _():
        if has_s0:
            s_ref[...] = s0_ref[0]
        else:
            s_ref[...] = jnp.zeros_like(s_ref)

    cos = cos_ref[...]
    sin = sin_ref[...]

    def pad(t):
        if T == Cp:
            return t
        return jnp.concatenate([t, jnp.zeros((Cp - T, t.shape[1]), t.dtype)], axis=0)

    for h in range(RET_HEADS):
        sl = slice(h * RET_DK, (h + 1) * RET_DK)
        q = pad(q_ref[0, :, sl].astype(F32))
        k = pad(k_ref[0, :, sl].astype(F32))
        v = pad(v_ref[0, :, sl].astype(F32)).astype(BF16)
        q = q * cos + pltpu.roll(q, RET_DK // 2, 1) * sin
        k = (k * cos + pltpu.roll(k, RET_DK // 2, 1) * sin) * (RET_DK ** -0.5)
        qb = q.astype(BF16)
        att = _dot_nt(qb, k.astype(BF16)) * inner_ref[h]
        s_prev = s_ref[h]
        o = _dot(att.astype(BF16), v) + _dot(qb, s_prev.astype(BF16)) * qd_ref[h]
        s_ref[h] = s_prev * cd_ref[h] + _dot_tn((k * kd_ref[h]).astype(BF16), v)
        mu = jnp.mean(o, axis=-1, keepdims=True)
        d = o - mu
        var = jnp.mean(d * d, axis=-1, keepdims=True)
        y = d * lax.rsqrt(var + GN_EPS) * gn_ref[:, sl]
        gate = g_ref[0, :, sl].astype(F32)
        gate = gate * jax.nn.sigmoid(gate)
        o_ref[0, :, sl] = (gate * y[:T]).astype(o_ref.dtype)

    @pl.when(c == nc - 1)
    def _():
        s_out_ref[0] = s_ref[...]


def _retention_tables(C, Cp):
    H = RET_HEADS
    lg = jnp.log1p(-jnp.exp2(-5.0 - jnp.arange(H, dtype=F32)))
    i = jnp.arange(C, dtype=F32)
    diff = i[:, None] - i[None, :]
    inner = jnp.where((diff >= 0)[None], jnp.exp(jnp.maximum(diff, 0.0)[None] * lg[:, None, None]), 0.0)
    qd = jnp.exp((i[None, :] + 1.0) * lg[:, None])
    kd = jnp.exp((C - 1.0 - i)[None, :] * lg[:, None])
    cd = jnp.exp(C * lg)
    inner = jnp.pad(inner, ((0, 0), (0, Cp - C), (0, Cp - C)))
    qd = jnp.broadcast_to(jnp.pad(qd, ((0, 0), (0, Cp - C)))[:, :, None], (H, Cp, LANES))
    kd = jnp.broadcast_to(jnp.pad(kd, ((0, 0), (0, Cp - C)))[:, :, None], (H, Cp, LANES))
    cd = jnp.broadcast_to(cd[:, None, None], (H, 1, LANES))
    return inner, qd, kd, cd


def _rope_tables(pos, rows):
    half = RET_DK // 2
    inv = ROPE_BASE ** (-jnp.arange(half, dtype=F32) / half)
    ang = pos.astype(F32)[:, None] * inv[None, :]
    cos = jnp.cos(ang)
    sin = jnp.sin(ang)
    cos_full = jnp.concatenate([cos, cos], axis=-1)
    sin_signed = jnp.concatenate([-sin, sin], axis=-1)
    padn = rows - pos.shape[0]
    return jnp.pad(cos_full, ((0, padn), (0, 0))), jnp.pad(sin_signed, ((0, padn), (0, 0)))


def _retention(z, pos, gn, s0, C, out_dtype):
    B, L, _ = z.shape
    n = L // C
    Cp = max(C, LANES)
    inner, qd, kd, cd = _retention_tables(C, Cp)
    cos, sin = _rope_tables(pos, n * Cp)
    has_s0 = s0 is not None
    H = RET_HEADS

    def seg(k):
        return pl.BlockSpec((1, C, HEAD_COLS), lambda b, c, k=k: (b, c, k))

    const3 = lambda b, c: (0, 0, 0)
    in_specs = [seg(0), seg(1), seg(2), seg(3),
                pl.BlockSpec((Cp, LANES), lambda b, c: (c, 0)),
                pl.BlockSpec((Cp, LANES), lambda b, c: (c, 0)),
                pl.BlockSpec((H, Cp, Cp), const3),
                pl.BlockSpec((H, Cp, LANES), const3),
                pl.BlockSpec((H, Cp, LANES), const3),
                pl.BlockSpec((H, 1, LANES), const3),
                pl.BlockSpec((1, HEAD_COLS), lambda b, c: (0, 0))]
    args = [z, z, z, z, cos, sin, inner, qd, kd, cd, gn.reshape(1, HEAD_COLS)]
    if has_s0:
        in_specs.append(pl.BlockSpec((1, H, RET_DK, RET_DV), lambda b, c: (b, 0, 0, 0)))
        args.append(s0)
    return pl.pallas_call(
        functools.partial(_retention_kernel, T=C, Cp=Cp, has_s0=has_s0),
        grid=(B, n),
        in_specs=in_specs,
        out_specs=[pl.BlockSpec((1, C, HEAD_COLS), lambda b, c: (b, c, 0)),
                   pl.BlockSpec((1, H, RET_DK, RET_DV), lambda b, c: (b, 0, 0, 0))],
        out_shape=[jax.ShapeDtypeStruct((B, L, HEAD_COLS), out_dtype),
                   jax.ShapeDtypeStruct((B, H, RET_DK, RET_DV), F32)],
        scratch_shapes=[pltpu.VMEM((H, RET_DK, RET_DV), F32)],
        compiler_params=_params(("parallel", "arbitrary")),
        name="retention",
    )(*args)


def _split3(x):
    hi = x.astype(BF16).astype(F32)
    r = x - hi
    mid = r.astype(BF16).astype(F32)
    lo = (r - mid).astype(BF16).astype(F32)
    return hi, mid, lo


def _lane_scan(x, stride, strict_suffix):
    n = x.shape[0]
    j = lax.broadcasted_iota(jnp.int32, (LANES, LANES), 0)
    s = lax.broadcasted_iota(jnp.int32, (LANES, LANES), 1)
    same = ((j - s) & (stride - 1)) == 0
    tri = jnp.logical_and(same, (j > s) if strict_suffix else (j <= s))
    rhs = jnp.concatenate([jnp.where(tri, 1.0, 0.0), jnp.where(same, 1.0, 0.0)], axis=1).astype(F32)
    hi, mid, lo = _split3(x)
    y = _dot(jnp.concatenate([hi, mid, lo], axis=0), rhs)
    y = y[0:n] + y[n:2 * n] + y[2 * n:3 * n]
    return y[:, :LANES], y[:, LANES:]


def _cumsum_kernel(x_ref, o_ref, *, n_chunks):
    carry = jnp.zeros((SUBLANES, LANES), F32)
    for c in range(n_chunks):
        sl = slice(c * LANES, (c + 1) * LANES)
        scan, tot = _lane_scan(x_ref[0, :, sl], 1, strict_suffix=False)
        o_ref[0, :, sl] = scan + carry
        carry = carry + tot


def _cumsum_lanes(x):
    B, H, N = x.shape
    return pl.pallas_call(
        functools.partial(_cumsum_kernel, n_chunks=N // LANES),
        grid=(B,),
        in_specs=[pl.BlockSpec((1, H, N), lambda b: (b, 0, 0))],
        out_specs=pl.BlockSpec((1, H, N), lambda b: (b, 0, 0)),
        out_shape=jax.ShapeDtypeStruct((B, H, N), F32),
        compiler_params=_params(("parallel",)),
        name="logf_cumsum",
    )(x)


def _fox_prompt_kernel(q_ref, k_ref, v_ref, cq_ref, ck_ref, o_ref, m_ref, l_ref, acc_ref, *, tq):
    qi = pl.program_id(1)
    ki = pl.program_id(2)
    kscale = (FOX_HD ** -0.5) * LOG2E

    @pl.when(ki == 0)
    def _():
        m_ref[...] = jnp.full_like(m_ref, -jnp.inf)
        l_ref[...] = jnp.zeros_like(l_ref)
        acc_ref[...] = jnp.zeros_like(acc_ref)

    def scores(h):
        sl = slice(h * FOX_HD, (h + 1) * FOX_HD)
        k = (k_ref[0, :, sl] * kscale).astype(BF16)
        return _dot_nt(q_ref[0, :, sl], k)

    def step(diagonal):
        if diagonal:
            row = lax.broadcasted_iota(jnp.int32, (tq, tq), 0)
            col = lax.broadcasted_iota(jnp.int32, (tq, tq), 1)
            causal = row >= col
        ck = ck_ref[0] * LOG2E
        s_next = scores(0)
        for h in range(FOX_HEADS):
            sl = slice(h * FOX_HD, (h + 1) * FOX_HD)
            s = s_next
            if h + 1 < FOX_HEADS:
                s_next = scores(h + 1)
            t = s - ck[h:h + 1, :]
            if diagonal:
                t = jnp.where(causal, t, -jnp.inf)
            cq = cq_ref[0, :, h:h + 1] * LOG2E
            m_prev = m_ref[h]
            m_new = jnp.maximum(m_prev, jnp.max(t, axis=1, keepdims=True) + cq)
            corr = jnp.exp2(m_prev - m_new)
            p = jnp.exp2(t - (m_new - cq))
            psum = p[:, :LANES]
            for c in range(1, tq // LANES):
                psum = psum + p[:, c * LANES:(c + 1) * LANES]
            l_ref[h] = l_ref[h] * corr + psum
            v = v_ref[0, :, sl].astype(BF16)
            acc_ref[:, sl] = acc_ref[:, sl] * corr + _dot(p.astype(BF16), v)
            m_ref[h] = m_new

    @pl.when(ki < qi)
    def _():
        step(False)

    @pl.when(ki == qi)
    def _():
        step(True)
        for h in range(FOX_HEADS):
            sl = slice(h * FOX_HD, (h + 1) * FOX_HD)
            l = jnp.sum(l_ref[h], axis=1, keepdims=True)
            o_ref[0, :, sl] = (acc_ref[:, sl] / l).astype(o_ref.dtype)


def _fox_prompt(zb, fk, fv, ct_col, ct_row, tq):
    B, L, _ = fk.shape
    n = L // tq
    kv_map = lambda b, qi, ki: (b, jnp.minimum(ki, qi), 0)
    return pl.pallas_call(
        functools.partial(_fox_prompt_kernel, tq=tq),
        grid=(B, n, n),
        in_specs=[pl.BlockSpec((1, tq, HEAD_COLS), lambda b, qi, ki: (b, qi, COL_FQ)),
                  pl.BlockSpec((1, tq, HEAD_COLS), kv_map),
                  pl.BlockSpec((1, tq, HEAD_COLS), kv_map),
                  pl.BlockSpec((1, tq, FOX_HEADS), lambda b, qi, ki: (b, qi, 0)),
                  pl.BlockSpec((1, FOX_HEADS, tq), lambda b, qi, ki: (b, 0, jnp.minimum(ki, qi)))],
        out_specs=pl.BlockSpec((1, tq, HEAD_COLS), lambda b, qi, ki: (b, qi, 0)),
        out_shape=jax.ShapeDtypeStruct((B, L, HEAD_COLS), BF16),
        scratch_shapes=[pltpu.VMEM((FOX_HEADS, tq, 1), F32),
                        pltpu.VMEM((FOX_HEADS, tq, LANES), F32),
                        pltpu.VMEM((tq, HEAD_COLS), F32)],
        compiler_params=_params(("parallel", "parallel", "arbitrary")),
        name="fox_prompt",
    )(zb, fk, fv, ct_col, ct_row)


def _mem_attend_kernel(q_ref, k_ref, v_ref, o_ref):
    scale = MEM_HD ** -0.5
    for h in range(MEM_HEADS):
        sl = slice(h * MEM_HD, (h + 1) * MEM_HD)
        q = q_ref[0, :, sl].astype(BF16)
        k = (k_ref[0, :, sl] * scale).astype(BF16)
        v = v_ref[0, :, sl].astype(BF16)
        s = _dot_nt(q, k)
        p = jnp.exp(s - jnp.max(s, axis=1, keepdims=True))
        o = _dot(p.astype(BF16), v) / jnp.sum(p, axis=1, keepdims=True)
        o_ref[0, :, sl] = o.astype(o_ref.dtype)


def _mem_attend(zb, mk, mv, tq, out_dtype):
    B, L, _ = zb.shape
    M = mk.shape[1]
    return pl.pallas_call(
        _mem_attend_kernel,
        grid=(B, L // tq),
        in_specs=[pl.BlockSpec((1, tq, HEAD_COLS), lambda b, i: (b, i, COL_MQ)),
                  pl.BlockSpec((1, M, HEAD_COLS), lambda b, i: (b, 0, 0)),
                  pl.BlockSpec((1, M, HEAD_COLS), lambda b, i: (b, 0, 0))],
        out_specs=pl.BlockSpec((1, tq, HEAD_COLS), lambda b, i: (b, i, 0)),
        out_shape=jax.ShapeDtypeStruct((B, L, HEAD_COLS), out_dtype),
        compiler_params=_params(("parallel", "parallel")),
        name="mem_attend",
    )(zb, mk, mv)


def _merge_kernel(x_ref, oret_ref, ofox_ref, omem_ref, gate_ref, wr_ref, wf_ref, wm_ref, wo_ref, o_ref, *, D):
    def branch(k, o_b, w_ref):
        g = jax.nn.sigmoid(gate_ref[:, k * D:(k + 1) * D].astype(F32))
        return g * _dot(o_b[...].astype(BF16), w_ref[...])

    merged = branch(0, oret_ref, wr_ref) + branch(1, ofox_ref, wf_ref) + branch(2, omem_ref, wm_ref)
    o_ref[...] = x_ref[...] + _dot(merged.astype(BF16), wo_ref[...])


def _merge(x2d, o_ret, o_fox, o_mem, zb, w_ret, w_fox, w_mem, w_o, tm):
    T, D = x2d.shape
    gate_blk = COL_GATES * HEAD_COLS // (N_BRANCH * D)
    assert gate_blk * N_BRANCH * D == COL_GATES * HEAD_COLS
    row = lambda i: (i, 0)
    const = lambda i: (0, 0)
    resident = functools.partial(pl.BlockSpec, index_map=const, pipeline_mode=pl.Buffered(1))
    return pl.pallas_call(
        functools.partial(_merge_kernel, D=D),
        grid=(T // tm,),
        in_specs=[pl.BlockSpec((tm, D), row),
                  pl.BlockSpec((tm, HEAD_COLS), row),
                  pl.BlockSpec((tm, HEAD_COLS), row),
                  pl.BlockSpec((tm, HEAD_COLS), row),
                  pl.BlockSpec((tm, N_BRANCH * D), lambda i: (i, gate_blk)),
                  resident((HEAD_COLS, D)),
                  resident((HEAD_COLS, D)),
                  resident((HEAD_COLS, D)),
                  resident((D, D))],
        out_specs=pl.BlockSpec((tm, D), row),
        out_shape=jax.ShapeDtypeStruct((T, D), F32),
        compiler_params=_params(("parallel",)),
        name="merge",
    )(x2d, o_ret, o_fox, o_mem, zb, w_ret, w_fox, w_mem, w_o)


def _mlp_kernel(x_ref, g_ref, wu_ref, wd_ref, gf_ref, o_ref, hn_ref, *, final):
    f = pl.program_id(1)
    nf = pl.num_programs(1)

    @pl.when(f == 0)
    def _():
        x = x_ref[...]
        hn_ref[...] = _rms_normed(x, g_ref[...]).astype(BF16)
        o_ref[...] = x

    u = jnp.maximum(_dot(hn_ref[...], wu_ref[...]), 0.0)
    o_ref[...] += _dot((u * u).astype(BF16), wd_ref[...])

    if final:
        @pl.when(f == nf - 1)
        def _():
            o_ref[...] = _rms_normed(o_ref[...], gf_ref[...])


def _mlp(x2d, g, w_up, w_down, g_final, tm, tf, final):
    T, D = x2d.shape
    Fd = w_up.shape[1]
    return pl.pallas_call(
        functools.partial(_mlp_kernel, final=final),
        grid=(T // tm, Fd // tf),
        in_specs=[pl.BlockSpec((tm, D), lambda i, f: (i, 0)),
                  pl.BlockSpec((1, D), lambda i, f: (0, 0)),
                  pl.BlockSpec((D, tf), lambda i, f: (0, f)),
                  pl.BlockSpec((tf, D), lambda i, f: (f, 0)),
                  pl.BlockSpec((1, D), lambda i, f: (0, 0))],
        out_specs=pl.BlockSpec((tm, D), lambda i, f: (i, 0)),
        out_shape=jax.ShapeDtypeStruct((T, D), F32),
        scratch_shapes=[pltpu.VMEM((tm, D), BF16)],
        compiler_params=_params(("parallel", "arbitrary")),
        name="mlp",
    )(x2d, g.reshape(1, D), w_up, w_down, g_final.reshape(1, D))


def _fox_new_tokens(q, kn, vn, lfn, qall_ref, ptcol_ref, m_ref, l_ref, acc_ref, carry_ref):
    H = FOX_HEADS
    R = q.shape[0]
    row = lax.broadcasted_iota(jnp.int32, (R, LANES), 0)
    lane = lax.broadcasted_iota(jnp.int32, (R, LANES), 1)
    qall = (q * (FOX_HD ** -0.5)).astype(BF16)
    qall_ref[...] = qall
    ptrow = _lane_scan(jnp.broadcast_to(lfn, (SUBLANES, LANES)), H, strict_suffix=False)[0][0:1]
    ptcol = jnp.sum(jnp.where(lane == row, ptrow, 0.0), axis=1, keepdims=True)
    ptcol_ref[...] = ptcol
    zpad = jnp.zeros((LANES - R, FOX_HD), F32)
    kn = jnp.concatenate([kn, zpad], axis=0).astype(BF16)
    vn = jnp.concatenate([vn, zpad], axis=0).astype(BF16)
    s = _dot_nt(qall, kn) + (ptcol - ptrow)
    causal = jnp.logical_and(((lane - row) & (H - 1)) == 0, (lane // H) <= (row // H))
    s = jnp.where(causal, s, -jnp.inf)
    m = jnp.max(s, axis=1, keepdims=True)
    p = jnp.exp(s - m)
    m_ref[...] = m
    l_ref[...] = jnp.sum(p, axis=1, keepdims=True)
    acc_ref[...] = _dot(p.astype(BF16), vn)
    carry_ref[...] = jnp.zeros_like(carry_ref)


def _fox_pages(k_pages, v_pages, lf_pages, qall_ref, ptcol_ref, m_ref, l_ref, acc_ref, carry_ref):
    H = FOX_HEADS
    G = len(k_pages)
    R = qall_ref.shape[0]
    row = lax.broadcasted_iota(jnp.int32, (R, LANES), 0)
    lane = lax.broadcasted_iota(jnp.int32, (R, LANES), 1)
    same_head = ((lane - row) & (H - 1)) == 0
    suf, rowtot = _lane_scan(jnp.concatenate(lf_pages, axis=0), H, strict_suffix=True)
    sub = lax.broadcasted_iota(jnp.int32, (SUBLANES, LANES), 0)
    qall = qall_ref[...]
    ptcol = ptcol_ref[...]
    carry = carry_ref[0:1, :]
    tiles = []
    tmax = None
    for i in range(G):
        pre = rowtot[i * SUBLANES:(i + 1) * SUBLANES]
        for sh in (1, 2, 4):
            pre = pre + jnp.where(sub >= sh, pltpu.roll(pre, sh, 0), 0.0)
        page_tot = pre[SUBLANES - 1:SUBLANES]
        later = suf[i * SUBLANES:(i + 1) * SUBLANES] + (page_tot - pre) + carry
        carry = carry + page_tot
        s2 = _dot(qall, k_pages[i][...].T.astype(BF16))
        for r in range(SUBLANES):
            t = jnp.where(same_head, s2[:, r * LANES:(r + 1) * LANES] + later[r:r + 1, :], -jnp.inf)
            tiles.append(t)
            tmax = t if tmax is None else jnp.maximum(tmax, t)
    carry_ref[...] = jnp.broadcast_to(carry, carry_ref.shape)
    m_prev = m_ref[...]
    m_new = jnp.maximum(m_prev, jnp.max(tmax, axis=1, keepdims=True) + ptcol)
    shift = m_new - ptcol
    corr = jnp.exp(m_prev - m_new)
    psum = jnp.zeros((R, LANES), F32)
    pv = jnp.zeros((R, FOX_HD), F32)
    for i in range(G):
        p = [jnp.exp(t - shift) for t in tiles[i * SUBLANES:(i + 1) * SUBLANES]]
        for pt in p:
            psum = psum + pt
        pv = pv + _dot(jnp.concatenate(p, axis=1).astype(BF16), v_pages[i][...].astype(BF16))
    l_ref[...] = l_ref[...] * corr + jnp.sum(psum, axis=1, keepdims=True)
    acc_ref[...] = acc_ref[...] * corr + pv
    m_ref[...] = m_new


def _page_copies(pt_ref, kc_ref, vc_ref, kbuf, vbuf, sem, u, slot, G, NG, layer):
    n_pages = NG * G
    bb = lax.div(u, NG)
    gg = lax.rem(u, NG)
    out = []
    for i in range(G):
        page = pt_ref[bb, n_pages - 1 - (gg * G + i)]
        out.append(pltpu.make_async_copy(kc_ref.at[layer, page], kbuf.at[slot, i], sem.at[0, slot]))
        out.append(pltpu.make_async_copy(vc_ref.at[layer, page], vbuf.at[slot, i], sem.at[1, slot]))
    return out


def _fox_sample_kernel(pt_ref, q_ref, kn_ref, vn_ref, lfn_ref, kc_ref, vc_ref, *refs, G, NG, layer):
    lf_refs = refs[:G]
    o_ref, kbuf, vbuf, sem, qall_ref, ptcol_ref, m_ref, l_ref, acc_ref, carry_ref = refs[G:]
    g = pl.program_id(1)
    n_slots = kbuf.shape[0]
    step = pl.program_id(0) * NG + g
    n_steps = pl.num_programs(0) * NG
    page_copies = functools.partial(_page_copies, pt_ref, kc_ref, vc_ref, kbuf, vbuf, sem,
                                    G=G, NG=NG, layer=layer)

    @pl.when(step == 0)
    def _():
        for j in range(n_slots - 1):
            for c in page_copies(j, j):
                c.start()

    ahead = step + (n_slots - 1)

    @pl.when(ahead < n_steps)
    def _():
        for c in page_copies(ahead, lax.rem(ahead, n_slots)):
            c.start()

    slot = lax.rem(step, n_slots)
    for c in page_copies(step, slot):
        c.wait()
    state = (qall_ref, ptcol_ref, m_ref, l_ref, acc_ref, carry_ref)

    @pl.when(g == 0)
    def _():
        _fox_new_tokens(q_ref[0], kn_ref[0], vn_ref[0], lfn_ref[0], *state)

    _fox_pages([kbuf.at[slot, i] for i in range(G)], [vbuf.at[slot, i] for i in range(G)],
               [lf_refs[i][0, 0] for i in range(G)], *state)

    @pl.when(g == NG - 1)
    def _():
        o_ref[0] = acc_ref[...] / l_ref[...]


def _fox_sample(fq, fk, fv, lf_new, cache_k, cache_v, cache_lf, page_table, layer, G):
    B, R, _ = fq.shape
    n_pages = page_table.shape[1]
    rows = cache_k.shape[2]
    assert rows == SUBLANES * LANES and n_pages % G == 0 and R <= LANES
    NG = n_pages // G
    tok = pl.BlockSpec((1, R, FOX_HD), lambda b, g, pt: (b, 0, 0))

    def lf_spec(i):
        return pl.BlockSpec((1, 1, SUBLANES, LANES),
                            lambda b, g, pt, i=i: (layer, pt[b, n_pages - 1 - (g * G + i)], 0, 0))

    assert B * NG >= PAGE_RING_SLOTS - 1
    in_specs = ([tok, tok, tok, pl.BlockSpec((1, 1, LANES), lambda b, g, pt: (b, 0, 0)),
                 pl.BlockSpec(memory_space=pl.ANY), pl.BlockSpec(memory_space=pl.ANY)]
                + [lf_spec(i) for i in range(G)])
    grid_spec = pltpu.PrefetchScalarGridSpec(
        num_scalar_prefetch=1,
        grid=(B, NG),
        in_specs=in_specs,
        out_specs=pl.BlockSpec((1, R, FOX_HD), lambda b, g, pt: (b, 0, 0)),
        scratch_shapes=[pltpu.VMEM((PAGE_RING_SLOTS, G, rows, FOX_HD), F32),
                        pltpu.VMEM((PAGE_RING_SLOTS, G, rows, FOX_HD), F32),
                        pltpu.SemaphoreType.DMA((2, PAGE_RING_SLOTS)),
                        pltpu.VMEM((R, FOX_HD), BF16),
                        pltpu.VMEM((R, 1), F32),
                        pltpu.VMEM((R, 1), F32),
                        pltpu.VMEM((R, 1), F32),
                        pltpu.VMEM((R, FOX_HD), F32),
                        pltpu.VMEM((SUBLANES, LANES), F32)])
    return pl.pallas_call(
        functools.partial(_fox_sample_kernel, G=G, NG=NG, layer=layer),
        grid_spec=grid_spec,
        out_shape=jax.ShapeDtypeStruct((B, R, FOX_HD), F32),
        compiler_params=_params(("arbitrary", "arbitrary")),
        name="fox_sample",
    )(page_table, fq, fk, fv, lf_new, cache_k, cache_v, *([cache_lf] * G))


def _mlp_fox_kernel(pt_ref, x_hbm, g_ref, wu_ref, wd_ref, gf_ref, q_ref, kn_ref, vn_ref, lfn_ref,
                    kc_ref, vc_ref, *refs, G, NG, units, layer, final):
    n_lf = units * G
    lf_refs = refs[:n_lf]
    (o_ref, fo_ref, hn_ref, xsem, kbuf, vbuf, sem,
     qall_ref, ptcol_ref, m_ref, l_ref, acc_ref, carry_ref) = refs[n_lf:]
    i = pl.program_id(0)
    f = pl.program_id(1)
    nf = pl.num_programs(1)
    step = i * nf + f
    n_units = pl.num_programs(0) * nf * units
    n_slots = kbuf.shape[0]
    tm = o_ref.shape[0]
    part = tm // units
    state = (qall_ref, ptcol_ref, m_ref, l_ref, acc_ref, carry_ref)
    page_copies = functools.partial(_page_copies, pt_ref, kc_ref, vc_ref, kbuf, vbuf, sem,
                                    G=G, NG=NG, layer=layer)

    @pl.when(f == 0)
    def _():
        cp = pltpu.make_async_copy(x_hbm.at[pl.ds(i * tm, tm)], o_ref, xsem.at[0])
        cp.start()
        cp.wait()
        hn_ref[...] = _rms_normed(o_ref[...], g_ref[...]).astype(BF16)

    @pl.when(step == 0)
    def _():
        for u0 in range(n_slots):
            for c in page_copies(u0, u0):
                c.start()

    for j in range(units):
        u = step * units + j
        slot = lax.rem(u, n_slots)
        b = lax.div(u, NG)
        if j == 0:
            @pl.when(lax.rem(u, NG) == 0)
            def _(b=b):
                _fox_new_tokens(q_ref[b], kn_ref[b], vn_ref[b], lfn_ref[b], *state)

        for c in page_copies(u, slot):
            c.wait()
        rows = slice(j * part, (j + 1) * part)
        h = jnp.maximum(_dot(hn_ref[rows], wu_ref[...]), 0.0)
        o_ref[rows] += _dot((h * h).astype(BF16), wd_ref[...])
        _fox_pages([kbuf.at[slot, p] for p in range(G)], [vbuf.at[slot, p] for p in range(G)],
                   [lf_refs[j * G + p][0, 0] for p in range(G)], *state)
        nxt = u + n_slots

        @pl.when(nxt < n_units)
        def _(nxt=nxt, slot=slot):
            for c in page_copies(nxt, slot):
                c.start()

        if j == units - 1:
            @pl.when(lax.rem(u, NG) == NG - 1)
            def _(b=b):
                fo_ref[b] = acc_ref[...] / l_ref[...]

    if final:
        @pl.when(f == nf - 1)
        def _():
            o_ref[...] = _rms_normed(o_ref[...], gf_ref[...])


def _mlp_fox(x2d, g, w_up, w_down, g_final, final, tm, tf, fq, fk, fv, lf_new, cache_k, cache_v, cache_lf,
             page_table, layer, G, units):
    T, D = x2d.shape
    Fd = w_up.shape[1]
    B, R, _ = fq.shape
    n_pages = page_table.shape[1]
    rows = cache_k.shape[2]
    NG = n_pages // G
    ni, nf = T // tm, Fd // tf
    assert rows == SUBLANES * LANES and NG * G == n_pages and R <= LANES
    assert NG % units == 0 and ni * nf * units == B * NG and tm % units == 0
    assert B * NG >= MLP_FOX_RING_SLOTS

    def lf_spec(j, p):
        def index_map(i, f, pt):
            u = (i * nf + f) * units + j
            return (layer, pt[lax.div(u, NG), n_pages - 1 - (lax.rem(u, NG) * G + p)], 0, 0)
        return pl.BlockSpec((1, 1, SUBLANES, LANES), index_map)

    whole = lambda shape: pl.BlockSpec(shape, lambda i, f, pt: (0,) * len(shape), pipeline_mode=pl.Buffered(1))
    any_space = pl.BlockSpec(memory_space=pl.ANY)
    in_specs = ([any_space,
                 pl.BlockSpec((1, D), lambda i, f, pt: (0, 0)),
                 pl.BlockSpec((D, tf), lambda i, f, pt: (0, f)),
                 pl.BlockSpec((tf, D), lambda i, f, pt: (f, 0)),
                 pl.BlockSpec((1, D), lambda i, f, pt: (0, 0)),
                 whole((B, R, FOX_HD)), whole((B, R, FOX_HD)), whole((B, R, FOX_HD)), whole((B, 1, LANES)),
                 any_space, any_space]
                + [lf_spec(j, p) for j in range(units) for p in range(G)])
    grid_spec = pltpu.PrefetchScalarGridSpec(
        num_scalar_prefetch=1,
        grid=(ni, nf),
        in_specs=in_specs,
        out_specs=[pl.BlockSpec((tm, D), lambda i, f, pt: (i, 0)),
                   pl.BlockSpec((B, R, FOX_HD), lambda i, f, pt: (0, 0, 0))],
        scratch_shapes=[pltpu.VMEM((tm, D), BF16),
                        pltpu.SemaphoreType.DMA((1,)),
                        pltpu.VMEM((MLP_FOX_RING_SLOTS, G, rows, FOX_HD), F32),
                        pltpu.VMEM((MLP_FOX_RING_SLOTS, G, rows, FOX_HD), F32),
                        pltpu.SemaphoreType.DMA((2, MLP_FOX_RING_SLOTS)),
                        pltpu.VMEM((R, FOX_HD), BF16),
                        pltpu.VMEM((R, 1), F32),
                        pltpu.VMEM((R, 1), F32),
                        pltpu.VMEM((R, 1), F32),
                        pltpu.VMEM((R, FOX_HD), F32),
                        pltpu.VMEM((SUBLANES, LANES), F32)])
    return pl.pallas_call(
        functools.partial(_mlp_fox_kernel, G=G, NG=NG, units=units, layer=layer, final=final),
        grid_spec=grid_spec,
        out_shape=[jax.ShapeDtypeStruct((T, D), F32), jax.ShapeDtypeStruct((B, R, FOX_HD), F32)],
        compiler_params=_params(("arbitrary", "arbitrary")),
        name="mlp_fox",
    )(page_table, x2d, g.reshape(1, D), w_up, w_down, g_final.reshape(1, D), fq, fk, fv, lf_new,
      cache_k, cache_v, *([cache_lf] * (units * G)))


def _pick(n, prefs):
    for p in prefs:
        if n % p == 0:
            return p
    return n


def _shared_tiling(mlp_steps, n_rows, n_pages):
    for G in (8, 4, 2, 1):
        for units in (2, 1, 4):
            if n_pages % G == 0 and (n_pages // G) % units == 0 and n_rows * (n_pages // G) == mlp_steps * units:
                return G, units
    return None


def _split_w_in(w_in, b_ff_l, layer):
    c = HEAD_COLS
    w_all = _regroup_cast(jnp.swapaxes(w_in, 1, 2), layer, _pick(w_in.shape[1], (512, 256)))
    w_ff = jnp.pad(w_in[layer, :, 7 * c:7 * c + FOX_HEADS], ((0, 0), (0, LANES - FOX_HEADS))).astype(BF16)
    b_ff = jnp.pad(b_ff_l.astype(F32), (0, LANES - FOX_HEADS)).reshape(1, LANES)
    return w_all, w_ff, b_ff


def kernel(x_prompt, x_sample, mem_prompt, state_ret, cache_fox_k, cache_fox_v, cache_fox_logf,
           cache_mem_k, cache_mem_v, page_table, norm_mix, w_in, b_fox_f, gn_ret, norm_mem, w_mem_kv,
           w_br_ret, w_br_fox, w_br_mem, w_o, norm_mlp, w_up, w_down, norm_final):
    B, L, D = x_prompt.shape
    Bd, T, _ = x_sample.shape
    depth = w_in.shape[0]
    n_pool, ps = cache_fox_k.shape[1], cache_fox_k.shape[2]
    n_pages = page_table.shape[1]
    M = mem_prompt.shape[1]
    H = FOX_HEADS
    past_len = n_pages * ps
    pos_p = jnp.arange(L, dtype=jnp.int32)
    pos_s = past_len + jnp.arange(T, dtype=jnp.int32)

    col_kv = COL_GATES + pl.cdiv(N_BRANCH * D, HEAD_COLS)
    n_zb = col_kv * HEAD_COLS
    tn = _pick(n_zb, (2048, 1024))
    tm_p = _pick(B * L, (1024, 512, 256))
    tm_kv = _pick(B * L, (512, 256))
    C_ret = _pick(L, (256, 128))
    tq_fox = _pick(L, (512, 256, 128))
    tq_mem = _pick(L, (512, 256, 128))
    tm_merge = _pick(B * L, (256,))
    tm_mlp = _pick(B * L, (1024, 512, 256))
    tf_mlp = _pick(w_up.shape[2], (512,))
    G = _pick(n_pages, (8, 4, 2))

    cache_k = cache_fox_k.reshape(depth, n_pool, ps * H, FOX_HD)
    cache_v = cache_fox_v.reshape(depth, n_pool, ps * H, FOX_HD)
    cache_lf = cache_fox_logf.reshape(depth, n_pool, ps * H // LANES, LANES)

    xp = x_prompt.reshape(B * L, D)
    xs = x_sample.reshape(Bd * T, D)
    outs = {k: [] for k in ("sp", "ss", "kp", "vp", "fp", "ks", "vs", "fs", "mk", "mv")}
    for l in range(depth):
        last = l == depth - 1
        w_all, w_ff, b_ff = _split_w_in(w_in, b_fox_f[l], l)
        w_ret = w_br_ret[l].astype(BF16)
        w_fox = w_br_fox[l].astype(BF16)
        w_mem = w_br_mem[l].astype(BF16)
        w_o_l = w_o[l].astype(BF16)
        w_up_l = w_up[l].astype(BF16)
        w_down_l = w_down[l].astype(BF16)

        zb = _norm_proj(xp, norm_mix[l], w_all, n_zb, tm_p, tn, BF16)
        fk, fv, logf = _norm_kv(xp, norm_mix[l], w_all, col_kv, tm_kv, w_ff, b_ff)
        zb3 = zb.reshape(B, L, zb.shape[1])
        o_ret, s_new = _retention(zb3, pos_p, gn_ret[l], None, C_ret, BF16)
        logf3 = logf[:, :H].reshape(B, L, H)
        ct_row = _cumsum_lanes(jnp.swapaxes(logf3, 1, 2))
        o_fox = _fox_prompt(zb3, fk.reshape(B, L, HEAD_COLS), fv.reshape(B, L, HEAD_COLS),
                            jnp.swapaxes(ct_row, 1, 2), ct_row, tq_fox)
        mk, mv = _norm_kv(mem_prompt.reshape(B * M, D), norm_mem[l], w_mem_kv[l].astype(BF16), 0,
                          _pick(B * M, (512, 256)))
        o_mem = _mem_attend(zb3, mk.reshape(B, M, HEAD_COLS), mv.reshape(B, M, HEAD_COLS), tq_mem, BF16)
        xp = _merge(xp, o_ret.reshape(B * L, HEAD_COLS), o_fox.reshape(B * L, HEAD_COLS),
                    o_mem.reshape(B * L, HEAD_COLS), zb, w_ret, w_fox, w_mem, w_o_l, tm_merge)
        zs = _norm_proj(xs, norm_mix[l], w_all, n_zb, Bd * T, tn, F32)
        fk_s, fv_s, logf_s = _norm_kv(xs, norm_mix[l], w_all, col_kv, Bd * T, w_ff, b_ff)
        logf3_s = logf_s[:, :H].reshape(Bd, T, H)
        fox_in = (zs[:, COL_FQ * HEAD_COLS:(COL_FQ + 1) * HEAD_COLS].reshape(Bd, T * H, FOX_HD),
                  fk_s.reshape(Bd, T * H, FOX_HD), fv_s.reshape(Bd, T * H, FOX_HD),
                  jnp.pad(logf3_s.reshape(Bd, 1, T * H), ((0, 0), (0, 0), (0, LANES - T * H))),
                  cache_k, cache_v, cache_lf, page_table, l)
        shared = _shared_tiling(B * L // tm_mlp * (w_up.shape[2] // tf_mlp), Bd, n_pages)
        if shared is None:
            xp = _mlp(xp, norm_mlp[l], w_up_l, w_down_l, norm_final, tm_mlp, tf_mlp, last)
            o_fox_s = _fox_sample(*fox_in, G)
        else:
            xp, o_fox_s = _mlp_fox(xp, norm_mlp[l], w_up_l, w_down_l, norm_final, last, tm_mlp, tf_mlp,
                                   *fox_in, *shared)
        outs["sp"].append(s_new)
        outs["kp"].append(fk.reshape(B, L, H, FOX_HD))
        outs["vp"].append(fv.reshape(B, L, H, FOX_HD))
        outs["fp"].append(logf3)
        outs["mk"].append(mk.reshape(B, M, MEM_HEADS, MEM_HD))
        outs["mv"].append(mv.reshape(B, M, MEM_HEADS, MEM_HD))

        zs3 = zs.reshape(Bd, T, zs.shape[1])
        o_ret, s_new = _retention(zs3, pos_s, gn_ret[l], state_ret[l], T, F32)
        o_mem = _mem_attend(zs3, cache_mem_k[l].reshape(Bd, M, HEAD_COLS),
                            cache_mem_v[l].reshape(Bd, M, HEAD_COLS), T, F32)
        xs = _merge(xs, o_ret.reshape(Bd * T, HEAD_COLS), o_fox_s.reshape(Bd * T, HEAD_COLS),
                    o_mem.reshape(Bd * T, HEAD_COLS), zs, w_ret, w_fox, w_mem, w_o_l, Bd * T)
        xs = _mlp(xs, norm_mlp[l], w_up_l, w_down_l, norm_final, Bd * T, tf_mlp, last)
        outs["ss"].append(s_new)
        outs["ks"].append(fk_s.reshape(Bd, T, H, FOX_HD))
        outs["vs"].append(fv_s.reshape(Bd, T, H, FOX_HD))
        outs["fs"].append(logf3_s)

    st = lambda k: jnp.stack(outs[k])
    return (xp.reshape(B, L, D), xs.reshape(Bd, T, D), st("sp"), st("ss"), st("kp"), st("vp"), st("fp"),
            st("ks"), st("vs"), st("fs"), st("mk"), st("mv"))
```

```python
import functools
import math

import jax
import jax.numpy as jnp
from jax import lax
from jax.experimental import pallas as pl
from jax.experimental.pallas import tpu as pltpu

F32 = jnp.float32
BF16 = jnp.bfloat16

RET_HEADS = 8
RET_DK = 128
RET_DV = 128
FOX_HEADS = 8
FOX_HD = 128
MEM_HEADS = 4
MEM_HD = 256
N_BRANCH = 3
ROPE_BASE = 10000.0
RMS_EPS = 1e-6
GN_EPS = 1e-5

HEAD_COLS = 1024
LANES = 128
SUBLANES = 8
VMEM_LIMIT = 56 * 1024 * 1024
LOG2E = math.log2(math.e)
PAGE_RING_SLOTS = 3
MLP_FOX_RING_SLOTS = 2

COL_FQ = 4
COL_MQ = 5
COL_GATES = 6


def _dot(a, b):
    return jnp.dot(a, b, preferred_element_type=F32)


def _dot_nt(a, b):
    return lax.dot_general(a, b, (((1,), (1,)), ((), ())), preferred_element_type=F32)


def _dot_tn(a, b):
    return lax.dot_general(a, b, (((0,), (0,)), ((), ())), preferred_element_type=F32)


def _params(sem):
    return pltpu.CompilerParams(dimension_semantics=sem, vmem_limit_bytes=VMEM_LIMIT)


def _rms_normed(x, g):
    ms = jnp.mean(x * x, axis=-1, keepdims=True)
    return x * lax.rsqrt(ms + RMS_EPS) * g


def _log_sigmoid(x):
    return jnp.minimum(x, 0.0) - jnp.log1p(jnp.exp(-jnp.abs(x)))


def _norm_proj_kernel(x_ref, g_ref, w_ref, o_ref, hn_ref):
    @pl.when(pl.program_id(1) == 0)
    def _():
        hn_ref[...] = _rms_normed(x_ref[...], g_ref[...]).astype(BF16)

    o_ref[...] = _dot(hn_ref[...], w_ref[...]).astype(o_ref.dtype)


def _norm_proj(x2d, g, w, n_cols, tm, tn, out_dtype):
    T, D = x2d.shape
    assert n_cols % tn == 0
    return pl.pallas_call(
        _norm_proj_kernel,
        grid=(T // tm, n_cols // tn),
        in_specs=[pl.BlockSpec((tm, D), lambda i, j: (i, 0)),
                  pl.BlockSpec((1, D), lambda i, j: (0, 0)),
                  pl.BlockSpec((D, tn), lambda i, j: (0, j))],
        out_specs=pl.BlockSpec((tm, tn), lambda i, j: (i, j)),
        out_shape=jax.ShapeDtypeStruct((T, n_cols), out_dtype),
        scratch_shapes=[pltpu.VMEM((tm, D), BF16)],
        compiler_params=_params(("parallel", "arbitrary")),
        name="norm_proj",
    )(x2d, g.reshape(1, D), w)


def _regroup_cast_kernel(a_ref, b_ref, o_ref, *, n_lead, n_gate_blocks, shift):
    j = pl.program_id(1)
    shifted = (j >= n_lead) & (j < n_lead + 1 + n_gate_blocks)

    @pl.when(jnp.logical_not(shifted))
    def _():
        o_ref[...] = a_ref[...].T.astype(BF16)

    @pl.when(shifted)
    def _():
        x = jnp.concatenate([a_ref[shift:, :], b_ref[:shift, :]], axis=0)
        o_ref[...] = x.T.astype(BF16)


def _regroup_cast(w_t, layer, tr):
    _, N, D = w_t.shape
    n_lead = 5
    n_gate_blocks = pl.cdiv(N_BRANCH * D, HEAD_COLS)
    n_blocks = n_lead + 1 + n_gate_blocks + 2
    last = pl.cdiv(N, HEAD_COLS) - 1

    def src(j):
        return jnp.where(j < n_lead, j, jnp.where(j < n_lead + 1 + n_gate_blocks, j + 2, j - 1 - n_gate_blocks))

    def src_next(j):
        shifted = (j >= n_lead) & (j < n_lead + 1 + n_gate_blocks)
        return jnp.where(shifted, jnp.minimum(src(j) + 1, last), n_lead + 3)

    return pl.pallas_call(
        functools.partial(_regroup_cast_kernel, n_lead=n_lead, n_gate_blocks=n_gate_blocks, shift=FOX_HEADS),
        grid=(D // tr, n_blocks),
        in_specs=[pl.BlockSpec((None, HEAD_COLS, tr), lambda r, j: (layer, src(j), r)),
                  pl.BlockSpec((None, HEAD_COLS, tr), lambda r, j: (layer, src_next(j), r))],
        out_specs=pl.BlockSpec((tr, HEAD_COLS), lambda r, j: (r, j)),
        out_shape=jax.ShapeDtypeStruct((D, n_blocks * HEAD_COLS), BF16),
        compiler_params=_params(("parallel", "parallel")),
        name="regroup_cast",
    )(w_t, w_t)


def _norm_kv_kernel(*refs, has_ff):
    if has_ff:
        x_ref, g_ref, wk_ref, wv_ref, wff_ref, bff_ref, k_ref, v_ref, logf_ref = refs
    else:
        x_ref, g_ref, wk_ref, wv_ref, k_ref, v_ref = refs
    hn = _rms_normed(x_ref[...], g_ref[...]).astype(BF16)
    k_ref[...] = _dot(hn, wk_ref[...])
    v_ref[...] = _dot(hn, wv_ref[...])
    if has_ff:
        logf_ref[...] = _log_sigmoid(_dot(hn, wff_ref[...]) + bff_ref[...])


def _norm_kv(x2d, g, w, col_block, tm, w_ff=None, b_ff=None):
    T, D = x2d.shape
    has_ff = w_ff is not None
    row = lambda i: (i, 0)
    const = lambda i: (0, 0)
    in_specs = [pl.BlockSpec((tm, D), row), pl.BlockSpec((1, D), const),
                pl.BlockSpec((D, HEAD_COLS), lambda i: (0, col_block), pipeline_mode=pl.Buffered(1)),
                pl.BlockSpec((D, HEAD_COLS), lambda i: (0, col_block + 1), pipeline_mode=pl.Buffered(1))]
    args = [x2d, g.reshape(1, D), w, w]
    out_specs = [pl.BlockSpec((tm, HEAD_COLS), row), pl.BlockSpec((tm, HEAD_COLS), row)]
    out_shape = [jax.ShapeDtypeStruct((T, HEAD_COLS), F32), jax.ShapeDtypeStruct((T, HEAD_COLS), F32)]
    if has_ff:
        in_specs += [pl.BlockSpec((D, LANES), const), pl.BlockSpec((1, LANES), const)]
        args += [w_ff, b_ff]
        out_specs.append(pl.BlockSpec((tm, LANES), row))
        out_shape.append(jax.ShapeDtypeStruct((T, LANES), F32))
    return pl.pallas_call(
        functools.partial(_norm_kv_kernel, has_ff=has_ff),
        grid=(T // tm,),
        in_specs=in_specs,
        out_specs=out_specs,
        out_shape=out_shape,
        compiler_params=_params(("parallel",)),
        name="norm_kv",
    )(*args)


def _retention_kernel(*refs, T, Cp, has_s0):
    q_ref, k_ref, v_ref, g_ref, cos_ref, sin_ref, inner_ref, qd_ref, kd_ref, cd_ref, gn_ref = refs[:11]
    pos = 11
    if has_s0:
        s0_ref = refs[pos]
        pos += 1
    o_ref, s_out_ref, s_ref = refs[pos:pos + 3]
    c = pl.program_id(1)
    nc = pl.num_programs(1)

    @pl.when(c == 0)
    def _():
        if has_s0:
            s_ref[...] = s0_ref[0]
        else:
            s_ref[...] = jnp.zeros_like(s_ref)

    cos = cos_ref[...]
    sin = sin_ref[...]

    def pad(t):
        if T == Cp:
            return t
        return jnp.concatenate([t, jnp.zeros((Cp - T, t.shape[1]), t.dtype)], axis=0)

    for h in range(RET_HEADS):
        sl = slice(h * RET_DK, (h + 1) * RET_DK)
        q = pad(q_ref[0, :, sl].astype(F32))
        k = pad(k_ref[0, :, sl].astype(F32))
        v = pad(v_ref[0, :, sl].astype(F32)).astype(BF16)
        q = q * cos + pltpu.roll(q, RET_DK // 2, 1) * sin
        k = (k * cos + pltpu.roll(k, RET_DK // 2, 1) * sin) * (RET_DK ** -0.5)
        qb = q.astype(BF16)
        att = _dot_nt(qb, k.astype(BF16)) * inner_ref[h]
        s_prev = s_ref[h]
        o = _dot(att.astype(BF16), v) + _dot(qb, s_prev.astype(BF16)) * qd_ref[h]
        s_ref[h] = s_prev * cd_ref[h] + _dot_tn((k * kd_ref[h]).astype(BF16), v)
        mu = jnp.mean(o, axis=-1, keepdims=True)
        d = o - mu
        var = jnp.mean(d * d, axis=-1, keepdims=True)
        y = d * lax.rsqrt(var + GN_EPS) * gn_ref[:, sl]
        gate = g_ref[0, :, sl].astype(F32)
        gate = gate * jax.nn.sigmoid(gate)
        o_ref[0, :, sl] = (gate * y[:T]).astype(o_ref.dtype)

    @pl.when(c == nc - 1)
    def _():
        s_out_ref[0] = s_ref[...]


def _retention_tables(C, Cp):
    H = RET_HEADS
    lg = jnp.log1p(-jnp.exp2(-5.0 - jnp.arange(H, dtype=F32)))
    i = jnp.arange(C, dtype=F32)
    diff = i[:, None] - i[None, :]
    inner = jnp.where((diff >= 0)[None], jnp.exp(jnp.maximum(diff, 0.0)[None] * lg[:, None, None]), 0.0)
    qd = jnp.exp((i[None, :] + 1.0) * lg[:, None])
    kd = jnp.exp((C - 1.0 - i)[None, :] * lg[:, None])
    cd = jnp.exp(C * lg)
    inner = jnp.pad(inner, ((0, 0), (0, Cp - C), (0, Cp - C)))
    qd = jnp.broadcast_to(jnp.pad(qd, ((0, 0), (0, Cp - C)))[:, :, None], (H, Cp, LANES))
    kd = jnp.broadcast_to(jnp.pad(kd, ((0, 0), (0, Cp - C)))[:, :, None], (H, Cp, LANES))
    cd = jnp.broadcast_to(cd[:, None, None], (H, 1, LANES))
    return inner, qd, kd, cd


def _rope_tables(pos, rows):
    half = RET_DK // 2
    inv = ROPE_BASE ** (-jnp.arange(half, dtype=F32) / half)
    ang = pos.astype(F32)[:, None] * inv[None, :]
    cos = jnp.cos(ang)
    sin = jnp.sin(ang)
    cos_full = jnp.concatenate([cos, cos], axis=-1)
    sin_signed = jnp.concatenate([-sin, sin], axis=-1)
    padn = rows - pos.shape[0]
    return jnp.pad(cos_full, ((0, padn), (0, 0))), jnp.pad(sin_signed, ((0, padn), (0, 0)))


def _retention(z, pos, gn, s0, C, out_dtype):
    B, L, _ = z.shape
    n = L // C
    Cp = max(C, LANES)
    inner, qd, kd, cd = _retention_tables(C, Cp)
    cos, sin = _rope_tables(pos, n * Cp)
    has_s0 = s0 is not None
    H = RET_HEADS

    def seg(k):
        return pl.BlockSpec((1, C, HEAD_COLS), lambda b, c, k=k: (b, c, k))

    const3 = lambda b, c: (0, 0, 0)
    in_specs = [seg(0), seg(1), seg(2), seg(3),
                pl.BlockSpec((Cp, LANES), lambda b, c: (c, 0)),
                pl.BlockSpec((Cp, LANES), lambda b, c: (c, 0)),
                pl.BlockSpec((H, Cp, Cp), const3),
                pl.BlockSpec((H, Cp, LANES), const3),
                pl.BlockSpec((H, Cp, LANES), const3),
                pl.BlockSpec((H, 1, LANES), const3),
                pl.BlockSpec((1, HEAD_COLS), lambda b, c: (0, 0))]
    args = [z, z, z, z, cos, sin, inner, qd, kd, cd, gn.reshape(1, HEAD_COLS)]
    if has_s0:
        in_specs.append(pl.BlockSpec((1, H, RET_DK, RET_DV), lambda b, c: (b, 0, 0, 0)))
        args.append(s0)
    return pl.pallas_call(
        functools.partial(_retention_kernel, T=C, Cp=Cp, has_s0=has_s0),
        grid=(B, n),
        in_specs=in_specs,
        out_specs=[pl.BlockSpec((1, C, HEAD_COLS), lambda b, c: (b, c, 0)),
                   pl.BlockSpec((1, H, RET_DK, RET_DV), lambda b, c: (b, 0, 0, 0))],
        out_shape=[jax.ShapeDtypeStruct((B, L, HEAD_COLS), out_dtype),
                   jax.ShapeDtypeStruct((B, H, RET_DK, RET_DV), F32)],
        scratch_shapes=[pltpu.VMEM((H, RET_DK, RET_DV), F32)],
        compiler_params=_params(("parallel", "arbitrary")),
        name="retention",
    )(*args)


def _split3(x):
    hi = x.astype(BF16).astype(F32)
    r = x - hi
    mid = r.astype(BF16).astype(F32)
    lo = (r - mid).astype(BF16).astype(F32)
    return hi, mid, lo


def _lane_scan(x, stride, strict_suffix):
    n = x.shape[0]
    j = lax.broadcasted_iota(jnp.int32, (LANES, LANES), 0)
    s = lax.broadcasted_iota(jnp.int32, (LANES, LANES), 1)
    same = ((j - s) & (stride - 1)) == 0
    tri = jnp.logical_and(same, (j > s) if strict_suffix else (j <= s))
    rhs = jnp.concatenate([jnp.where(tri, 1.0, 0.0), jnp.where(same, 1.0, 0.0)], axis=1).astype(F32)
    hi, mid, lo = _split3(x)
    y = _dot(jnp.concatenate([hi, mid, lo], axis=0), rhs)
    y = y[0:n] + y[n:2 * n] + y[2 * n:3 * n]
    return y[:, :LANES], y[:, LANES:]


def _cumsum_kernel(x_ref, o_ref, *, n_chunks):
    carry = jnp.zeros((SUBLANES, LANES), F32)
    for c in range(n_chunks):
        sl = slice(c * LANES, (c + 1) * LANES)
        scan, tot = _lane_scan(x_ref[0, :, sl], 1, strict_suffix=False)
        o_ref[0, :, sl] = scan + carry
        carry = carry + tot


def _cumsum_lanes(x):
    B, H, N = x.shape
    return pl.pallas_call(
        functools.partial(_cumsum_kernel, n_chunks=N // LANES),
        grid=(B,),
        in_specs=[pl.BlockSpec((1, H, N), lambda b: (b, 0, 0))],
        out_specs=pl.BlockSpec((1, H, N), lambda b: (b, 0, 0)),
        out_shape=jax.ShapeDtypeStruct((B, H, N), F32),
        compiler_params=_params(("parallel",)),
        name="logf_cumsum",
    )(x)


def _fox_prompt_kernel(q_ref, k_ref, v_ref, cq_ref, ck_ref, o_ref, m_ref, l_ref, acc_ref, *, tq):
    qi = pl.program_id(1)
    ki = pl.program_id(2)
    kscale = (FOX_HD ** -0.5) * LOG2E

    @pl.when(ki == 0)
    def _():
        m_ref[...] = jnp.full_like(m_ref, -jnp.inf)
        l_ref[...] = jnp.zeros_like(l_ref)
        acc_ref[...] = jnp.zeros_like(acc_ref)

    def scores(h):
        sl = slice(h * FOX_HD, (h + 1) * FOX_HD)
        k = (k_ref[0, :, sl] * kscale).astype(BF16)
        return _dot_nt(k, q_ref[0, :, sl])

    def step(diagonal):
        if diagonal:
            key = lax.broadcasted_iota(jnp.int32, (tq, tq), 0)
            qry = lax.broadcasted_iota(jnp.int32, (tq, tq), 1)
            causal = key <= qry
        s_next = scores(0)
        for h in range(FOX_HEADS):
            sl = slice(h * FOX_HD, (h + 1) * FOX_HD)
            s = s_next
            if h + 1 < FOX_HEADS:
                s_next = scores(h + 1)
            ck = ck_ref[0, h] * LOG2E
            t = jnp.concatenate([s[:, c * LANES:(c + 1) * LANES] - ck for c in range(tq // LANES)], axis=1)
            if diagonal:
                t = jnp.where(causal, t, -jnp.inf)
            cq = cq_ref[0, h:h + 1, :] * LOG2E
            m_prev = m_ref[h]
            m_new = jnp.maximum(m_prev, jnp.max(t, axis=0, keepdims=True) + cq)
            corr = jnp.exp2(m_prev - m_new)
            p = jnp.exp2(t - (m_new - cq))
            l_ref[h] = l_ref[h] * corr + jnp.sum(p, axis=0, keepdims=True)
            v = v_ref[0, :, sl].astype(BF16)
            acc_ref[h] = acc_ref[h] * corr + _dot_tn(v, p.astype(BF16))
            m_ref[h] = m_new

    @pl.when(ki < qi)
    def _():
        step(False)

    @pl.when(ki == qi)
    def _():
        step(True)
        for h in range(FOX_HEADS):
            sl = slice(h * FOX_HD, (h + 1) * FOX_HD)
            o_ref[0, :, sl] = (acc_ref[h] / l_ref[h]).T.astype(o_ref.dtype)


def _fox_prompt(zb, fk, fv, ct_row, ct_lanes, tq):
    B, L, _ = fk.shape
    n = L // tq
    kv_map = lambda b, qi, ki: (b, jnp.minimum(ki, qi), 0)
    return pl.pallas_call(
        functools.partial(_fox_prompt_kernel, tq=tq),
        grid=(B, n, n),
        in_specs=[pl.BlockSpec((1, tq, HEAD_COLS), lambda b, qi, ki: (b, qi, COL_FQ)),
                  pl.BlockSpec((1, tq, HEAD_COLS), kv_map),
                  pl.BlockSpec((1, tq, HEAD_COLS), kv_map),
                  pl.BlockSpec((1, FOX_HEADS, tq), lambda b, qi, ki: (b, 0, qi)),
                  pl.BlockSpec((1, FOX_HEADS, tq, LANES), lambda b, qi, ki: (b, 0, jnp.minimum(ki, qi), 0))],
        out_specs=pl.BlockSpec((1, tq, HEAD_COLS), lambda b, qi, ki: (b, qi, 0)),
        out_shape=jax.ShapeDtypeStruct((B, L, HEAD_COLS), BF16),
        scratch_shapes=[pltpu.VMEM((FOX_HEADS, 1, tq), F32),
                        pltpu.VMEM((FOX_HEADS, 1, tq), F32),
                        pltpu.VMEM((FOX_HEADS, FOX_HD, tq), F32)],
        compiler_params=_params(("parallel", "parallel", "arbitrary")),
        name="fox_prompt",
    )(zb, fk, fv, ct_row, ct_lanes)


def _mem_attend_kernel(q_ref, k_ref, v_ref, o_ref):
    scale = MEM_HD ** -0.5
    for h in range(MEM_HEADS):
        sl = slice(h * MEM_HD, (h + 1) * MEM_HD)
        q = q_ref[0, :, sl].astype(BF16)
        k = (k_ref[0, :, sl] * scale).astype(BF16)
        v = v_ref[0, :, sl].astype(BF16)
        s = _dot_nt(q, k)
        p = jnp.exp(s - jnp.max(s, axis=1, keepdims=True))
        o = _dot(p.astype(BF16), v) / jnp.sum(p, axis=1, keepdims=True)
        o_ref[0, :, sl] = o.astype(o_ref.dtype)


def _mem_attend(zb, mk, mv, tq, out_dtype):
    B, L, _ = zb.shape
    M = mk.shape[1]
    return pl.pallas_call(
        _mem_attend_kernel,
        grid=(B, L // tq),
        in_specs=[pl.BlockSpec((1, tq, HEAD_COLS), lambda b, i: (b, i, COL_MQ)),
                  pl.BlockSpec((1, M, HEAD_COLS), lambda b, i: (b, 0, 0)),
                  pl.BlockSpec((1, M, HEAD_COLS), lambda b, i: (b, 0, 0))],
        out_specs=pl.BlockSpec((1, tq, HEAD_COLS), lambda b, i: (b, i, 0)),
        out_shape=jax.ShapeDtypeStruct((B, L, HEAD_COLS), out_dtype),
        compiler_params=_params(("parallel", "parallel")),
        name="mem_attend",
    )(zb, mk, mv)


def _merge_kernel(x_ref, oret_ref, ofox_ref, omem_ref, gate_ref, wr_ref, wf_ref, wm_ref, wo_ref, o_ref, *, D):
    def branch(k, o_b, w_ref):
        g = jax.nn.sigmoid(gate_ref[:, k * D:(k + 1) * D].astype(F32))
        return g * _dot(o_b[...].astype(BF16), w_ref[...])

    merged = branch(0, oret_ref, wr_ref) + branch(1, ofox_ref, wf_ref) + branch(2, omem_ref, wm_ref)
    o_ref[...] = x_ref[...] + _dot(merged.astype(BF16), wo_ref[...])


def _merge(x2d, o_ret, o_fox, o_mem, zb, w_ret, w_fox, w_mem, w_o, tm):
    T, D = x2d.shape
    gate_blk = COL_GATES * HEAD_COLS // (N_BRANCH * D)
    assert gate_blk * N_BRANCH * D == COL_GATES * HEAD_COLS
    row = lambda i: (i, 0)
    const = lambda i: (0, 0)
    resident = functools.partial(pl.BlockSpec, index_map=const, pipeline_mode=pl.Buffered(1))
    return pl.pallas_call(
        functools.partial(_merge_kernel, D=D),
        grid=(T // tm,),
        in_specs=[pl.BlockSpec((tm, D), row),
                  pl.BlockSpec((tm, HEAD_COLS), row),
                  pl.BlockSpec((tm, HEAD_COLS), row),
                  pl.BlockSpec((tm, HEAD_COLS), row),
                  pl.BlockSpec((tm, N_BRANCH * D), lambda i: (i, gate_blk)),
                  resident((HEAD_COLS, D)),
                  resident((HEAD_COLS, D)),
                  resident((HEAD_COLS, D)),
                  resident((D, D))],
        out_specs=pl.BlockSpec((tm, D), row),
        out_shape=jax.ShapeDtypeStruct((T, D), F32),
        compiler_params=_params(("parallel",)),
        name="merge",
    )(x2d, o_ret, o_fox, o_mem, zb, w_ret, w_fox, w_mem, w_o)


def _mlp_kernel(x_ref, g_ref, wu_ref, wd_ref, gf_ref, o_ref, hn_ref, *, final):
    f = pl.program_id(1)
    nf = pl.num_programs(1)

    @pl.when(f == 0)
    def _():
        x = x_ref[...]
        hn_ref[...] = _rms_normed(x, g_ref[...]).astype(BF16)
        o_ref[...] = x

    u = jnp.maximum(_dot(hn_ref[...], wu_ref[...]), 0.0)
    o_ref[...] += _dot((u * u).astype(BF16), wd_ref[...])

    if final:
        @pl.when(f == nf - 1)
        def _():
            o_ref[...] = _rms_normed(o_ref[...], gf_ref[...])


def _mlp(x2d, g, w_up, w_down, g_final, tm, tf, final):
    T, D = x2d.shape
    Fd = w_up.shape[1]
    return pl.pallas_call(
        functools.partial(_mlp_kernel, final=final),
        grid=(T // tm, Fd // tf),
        in_specs=[pl.BlockSpec((tm, D), lambda i, f: (i, 0)),
                  pl.BlockSpec((1, D), lambda i, f: (0, 0)),
                  pl.BlockSpec((D, tf), lambda i, f: (0, f)),
                  pl.BlockSpec((tf, D), lambda i, f: (f, 0)),
                  pl.BlockSpec((1, D), lambda i, f: (0, 0))],
        out_specs=pl.BlockSpec((tm, D), lambda i, f: (i, 0)),
        out_shape=jax.ShapeDtypeStruct((T, D), F32),
        scratch_shapes=[pltpu.VMEM((tm, D), BF16)],
        compiler_params=_params(("parallel", "arbitrary")),
        name="mlp",
    )(x2d, g.reshape(1, D), w_up, w_down, g_final.reshape(1, D))


def _fox_new_tokens(q, kn, vn, lfn, qall_ref, ptcol_ref, m_ref, l_ref, acc_ref, carry_ref):
    H = FOX_HEADS
    R = q.shape[0]
    row = lax.broadcasted_iota(jnp.int32, (R, LANES), 0)
    lane = lax.broadcasted_iota(jnp.int32, (R, LANES), 1)
    qall = (q * (FOX_HD ** -0.5)).astype(BF16)
    qall_ref[...] = qall
    ptrow = _lane_scan(jnp.broadcast_to(lfn, (SUBLANES, LANES)), H, strict_suffix=False)[0][0:1]
    ptcol = jnp.sum(jnp.where(lane == row, ptrow, 0.0), axis=1, keepdims=True)
    ptcol_ref[...] = ptcol
    zpad = jnp.zeros((LANES - R, FOX_HD), F32)
    kn = jnp.concatenate([kn, zpad], axis=0).astype(BF16)
    vn = jnp.concatenate([vn, zpad], axis=0).astype(BF16)
    s = _dot_nt(qall, kn) + (ptcol - ptrow)
    causal = jnp.logical_and(((lane - row) & (H - 1)) == 0, (lane // H) <= (row // H))
    s = jnp.where(causal, s, -jnp.inf)
    m = jnp.max(s, axis=1, keepdims=True)
    p = jnp.exp(s - m)
    m_ref[...] = m
    l_ref[...] = jnp.sum(p, axis=1, keepdims=True)
    acc_ref[...] = _dot(p.astype(BF16), vn)
    carry_ref[...] = jnp.zeros_like(carry_ref)


def _fox_pages(k_pages, v_pages, lf_pages, qall_ref, ptcol_ref, m_ref, l_ref, acc_ref, carry_ref):
    H = FOX_HEADS
    G = len(k_pages)
    R = qall_ref.shape[0]
    row = lax.broadcasted_iota(jnp.int32, (R, LANES), 0)
    lane = lax.broadcasted_iota(jnp.int32, (R, LANES), 1)
    same_head = ((lane - row) & (H - 1)) == 0
    suf, rowtot = _lane_scan(jnp.concatenate(lf_pages, axis=0), H, strict_suffix=True)
    sub = lax.broadcasted_iota(jnp.int32, (SUBLANES, LANES), 0)
    qall = qall_ref[...]
    ptcol = ptcol_ref[...]
    carry = carry_ref[0:1, :]
    tiles = []
    tmax = None
    for i in range(G):
        pre = rowtot[i * SUBLANES:(i + 1) * SUBLANES]
        for sh in (1, 2, 4):
            pre = pre + jnp.where(sub >= sh, pltpu.roll(pre, sh, 0), 0.0)
        page_tot = pre[SUBLANES - 1:SUBLANES]
        later = suf[i * SUBLANES:(i + 1) * SUBLANES] + (page_tot - pre) + carry
        carry = carry + page_tot
        s2 = _dot(qall, k_pages[i][...].T.astype(BF16))
        for r in range(SUBLANES):
            t = jnp.where(same_head, s2[:, r * LANES:(r + 1) * LANES] + later[r:r + 1, :], -jnp.inf)
            tiles.append(t)
            tmax = t if tmax is None else jnp.maximum(tmax, t)
    carry_ref[...] = jnp.broadcast_to(carry, carry_ref.shape)
    m_prev = m_ref[...]
    m_new = jnp.maximum(m_prev, jnp.max(tmax, axis=1, keepdims=True) + ptcol)
    shift = m_new - ptcol
    corr = jnp.exp(m_prev - m_new)
    psum = jnp.zeros((R, LANES), F32)
    pv = jnp.zeros((R, FOX_HD), F32)
    for i in range(G):
        p = [jnp.exp(t - shift) for t in tiles[i * SUBLANES:(i + 1) * SUBLANES]]
        for pt in p:
            psum = psum + pt
        pv = pv + _dot(jnp.concatenate(p, axis=1).astype(BF16), v_pages[i][...].astype(BF16))
    l_ref[...] = l_ref[...] * corr + jnp.sum(psum, axis=1, keepdims=True)
    acc_ref[...] = acc_ref[...] * corr + pv
    m_ref[...] = m_new


def _page_copies(pt_ref, kc_ref, vc_ref, kbuf, vbuf, sem, u, slot, G, NG, layer):
    n_pages = NG * G
    bb = lax.div(u, NG)
    gg = lax.rem(u, NG)
    out = []
    for i in range(G):
        page = pt_ref[bb, n_pages - 1 - (gg * G + i)]
        out.append(pltpu.make_async_copy(kc_ref.at[layer, page], kbuf.at[slot, i], sem.at[0, slot]))
        out.append(pltpu.make_async_copy(vc_ref.at[layer, page], vbuf.at[slot, i], sem.at[1, slot]))
    return out


def _fox_sample_kernel(pt_ref, q_ref, kn_ref, vn_ref, lfn_ref, kc_ref, vc_ref, *refs, G, NG, layer):
    lf_refs = refs[:G]
    o_ref, kbuf, vbuf, sem, qall_ref, ptcol_ref, m_ref, l_ref, acc_ref, carry_ref = refs[G:]
    g = pl.program_id(1)
    n_slots = kbuf.shape[0]
    step = pl.program_id(0) * NG + g
    n_steps = pl.num_programs(0) * NG
    page_copies = functools.partial(_page_copies, pt_ref, kc_ref, vc_ref, kbuf, vbuf, sem,
                                    G=G, NG=NG, layer=layer)

    @pl.when(step == 0)
    def _():
        for j in range(n_slots - 1):
            for c in page_copies(j, j):
                c.start()

    ahead = step + (n_slots - 1)

    @pl.when(ahead < n_steps)
    def _():
        for c in page_copies(ahead, lax.rem(ahead, n_slots)):
            c.start()

    slot = lax.rem(step, n_slots)
    for c in page_copies(step, slot):
        c.wait()
    state = (qall_ref, ptcol_ref, m_ref, l_ref, acc_ref, carry_ref)

    @pl.when(g == 0)
    def _():
        _fox_new_tokens(q_ref[0], kn_ref[0], vn_ref[0], lfn_ref[0], *state)

    _fox_pages([kbuf.at[slot, i] for i in range(G)], [vbuf.at[slot, i] for i in range(G)],
               [lf_refs[i][0, 0] for i in range(G)], *state)

    @pl.when(g == NG - 1)
    def _():
        o_ref[0] = acc_ref[...] / l_ref[...]


def _fox_sample(fq, fk, fv, lf_new, cache_k, cache_v, cache_lf, page_table, layer, G):
    B, R, _ = fq.shape
    n_pages = page_table.shape[1]
    rows = cache_k.shape[2]
    assert rows == SUBLANES * LANES and n_pages % G == 0 and R <= LANES
    NG = n_pages // G
    tok = pl.BlockSpec((1, R, FOX_HD), lambda b, g, pt: (b, 0, 0))

    def lf_spec(i):
        return pl.BlockSpec((1, 1, SUBLANES, LANES),
                            lambda b, g, pt, i=i: (layer, pt[b, n_pages - 1 - (g * G + i)], 0, 0))

    assert B * NG >= PAGE_RING_SLOTS - 1
    in_specs = ([tok, tok, tok, pl.BlockSpec((1, 1, LANES), lambda b, g, pt: (b, 0, 0)),
                 pl.BlockSpec(memory_space=pl.ANY), pl.BlockSpec(memory_space=pl.ANY)]
                + [lf_spec(i) for i in range(G)])
    grid_spec = pltpu.PrefetchScalarGridSpec(
        num_scalar_prefetch=1,
        grid=(B, NG),
        in_specs=in_specs,
        out_specs=pl.BlockSpec((1, R, FOX_HD), lambda b, g, pt: (b, 0, 0)),
        scratch_shapes=[pltpu.VMEM((PAGE_RING_SLOTS, G, rows, FOX_HD), F32),
                        pltpu.VMEM((PAGE_RING_SLOTS, G, rows, FOX_HD), F32),
                        pltpu.SemaphoreType.DMA((2, PAGE_RING_SLOTS)),
                        pltpu.VMEM((R, FOX_HD), BF16),
                        pltpu.VMEM((R, 1), F32),
                        pltpu.VMEM((R, 1), F32),
                        pltpu.VMEM((R, 1), F32),
                        pltpu.VMEM((R, FOX_HD), F32),
                        pltpu.VMEM((SUBLANES, LANES), F32)])
    return pl.pallas_call(
        functools.partial(_fox_sample_kernel, G=G, NG=NG, layer=layer),
        grid_spec=grid_spec,
        out_shape=jax.ShapeDtypeStruct((B, R, FOX_HD), F32),
        compiler_params=_params(("arbitrary", "arbitrary")),
        name="fox_sample",
    )(page_table, fq, fk, fv, lf_new, cache_k, cache_v, *([cache_lf] * G))


def _mlp_fox_kernel(pt_ref, x_hbm, g_ref, wu_ref, wd_ref, gf_ref, q_ref, kn_ref, vn_ref, lfn_ref,
                    kc_ref, vc_ref, *refs, G, NG, units, layer, final):
    n_lf = units * G
    lf_refs = refs[:n_lf]
    (o_ref, fo_ref, hn_ref, xsem, kbuf, vbuf, sem,
     qall_ref, ptcol_ref, m_ref, l_ref, acc_ref, carry_ref) = refs[n_lf:]
    i = pl.program_id(0)
    f = pl.program_id(1)
    nf = pl.num_programs(1)
    step = i * nf + f
    n_units = pl.num_programs(0) * nf * units
    n_slots = kbuf.shape[0]
    tm = o_ref.shape[0]
    part = tm // units
    state = (qall_ref, ptcol_ref, m_ref, l_ref, acc_ref, carry_ref)
    page_copies = functools.partial(_page_copies, pt_ref, kc_ref, vc_ref, kbuf, vbuf, sem,
                                    G=G, NG=NG, layer=layer)

    @pl.when(f == 0)
    def _():
        cp = pltpu.make_async_copy(x_hbm.at[pl.ds(i * tm, tm)], o_ref, xsem.at[0])
        cp.start()
        cp.wait()
        hn_ref[...] = _rms_normed(o_ref[...], g_ref[...]).astype(BF16)

    @pl.when(step == 0)
    def _():
        for u0 in range(n_slots):
            for c in page_copies(u0, u0):
                c.start()

    for j in range(units):
        u = step * units + j
        slot = lax.rem(u, n_slots)
        b = lax.div(u, NG)
        if j == 0:
            @pl.when(lax.rem(u, NG) == 0)
            def _(b=b):
                _fox_new_tokens(q_ref[b], kn_ref[b], vn_ref[b], lfn_ref[b], *state)

        for c in page_copies(u, slot):
            c.wait()
        rows = slice(j * part, (j + 1) * part)
        h = jnp.maximum(_dot(hn_ref[rows], wu_ref[...]), 0.0)
        o_ref[rows] += _dot((h * h).astype(BF16), wd_ref[...])
        _fox_pages([kbuf.at[slot, p] for p in range(G)], [vbuf.at[slot, p] for p in range(G)],
                   [lf_refs[j * G + p][0, 0] for p in range(G)], *state)
        nxt = u + n_slots

        @pl.when(nxt < n_units)
        def _(nxt=nxt, slot=slot):
            for c in page_copies(nxt, slot):
                c.start()

        if j == units - 1:
            @pl.when(lax.rem(u, NG) == NG - 1)
            def _(b=b):
                fo_ref[b] = acc_ref[...] / l_ref[...]

    if final:
        @pl.when(f == nf - 1)
        def _():
            o_ref[...] = _rms_normed(o_ref[...], gf_ref[...])


def _mlp_fox(x2d, g, w_up, w_down, g_final, final, tm, tf, fq, fk, fv, lf_new, cache_k, cache_v, cache_lf,
             page_table, layer, G, units):
    T, D = x2d.shape
    Fd = w_up.shape[1]
    B, R, _ = fq.shape
    n_pages = page_table.shape[1]
    rows = cache_k.shape[2]
    NG = n_pages // G
    ni, nf = T // tm, Fd // tf
    assert rows == SUBLANES * LANES and NG * G == n_pages and R <= LANES
    assert NG % units == 0 and ni * nf * units == B * NG and tm % units == 0
    assert B * NG >= MLP_FOX_RING_SLOTS

    def lf_spec(j, p):
        def index_map(i, f, pt):
            u = (i * nf + f) * units + j
            return (layer, pt[lax.div(u, NG), n_pages - 1 - (lax.rem(u, NG) * G + p)], 0, 0)
        return pl.BlockSpec((1, 1, SUBLANES, LANES), index_map)

    whole = lambda shape: pl.BlockSpec(shape, lambda i, f, pt: (0,) * len(shape), pipeline_mode=pl.Buffered(1))
    any_space = pl.BlockSpec(memory_space=pl.ANY)
    in_specs = ([any_space,
                 pl.BlockSpec((1, D), lambda i, f, pt: (0, 0)),
                 pl.BlockSpec((D, tf), lambda i, f, pt: (0, f)),
                 pl.BlockSpec((tf, D), lambda i, f, pt: (f, 0)),
                 pl.BlockSpec((1, D), lambda i, f, pt: (0, 0)),
                 whole((B, R, FOX_HD)), whole((B, R, FOX_HD)), whole((B, R, FOX_HD)), whole((B, 1, LANES)),
                 any_space, any_space]
                + [lf_spec(j, p) for j in range(units) for p in range(G)])
    grid_spec = pltpu.PrefetchScalarGridSpec(
        num_scalar_prefetch=1,
        grid=(ni, nf),
        in_specs=in_specs,
        out_specs=[pl.BlockSpec((tm, D), lambda i, f, pt: (i, 0)),
                   pl.BlockSpec((B, R, FOX_HD), lambda i, f, pt: (0, 0, 0))],
        scratch_shapes=[pltpu.VMEM((tm, D), BF16),
                        pltpu.SemaphoreType.DMA((1,)),
                        pltpu.VMEM((MLP_FOX_RING_SLOTS, G, rows, FOX_HD), F32),
                        pltpu.VMEM((MLP_FOX_RING_SLOTS, G, rows, FOX_HD), F32),
                        pltpu.SemaphoreType.DMA((2, MLP_FOX_RING_SLOTS)),
                        pltpu.VMEM((R, FOX_HD), BF16),
                        pltpu.VMEM((R, 1), F32),
                        pltpu.VMEM((R, 1), F32),
                        pltpu.VMEM((R, 1), F32),
                        pltpu.VMEM((R, FOX_HD), F32),
                        pltpu.VMEM((SUBLANES, LANES), F32)])
    return pl.pallas_call(
        functools.partial(_mlp_fox_kernel, G=G, NG=NG, units=units, layer=layer, final=final),
        grid_spec=grid_spec,
        out_shape=[jax.ShapeDtypeStruct((T, D), F32), jax.ShapeDtypeStruct((B, R, FOX_HD), F32)],
        compiler_params=_params(("arbitrary", "arbitrary")),
        name="mlp_fox",
    )(page_table, x2d, g.reshape(1, D), w_up, w_down, g_final.reshape(1, D), fq, fk, fv, lf_new,
      cache_k, cache_v, *([cache_lf] * (units * G)))


def _pick(n, prefs):
    for p in prefs:
        if n % p == 0:
            return p
    return n


def _shared_tiling(mlp_steps, n_rows, n_pages):
    for G in (8, 4, 2, 1):
        for units in (2, 1, 4):
            if n_pages % G == 0 and (n_pages // G) % units == 0 and n_rows * (n_pages // G) == mlp_steps * units:
                return G, units
    return None


def _split_w_in(w_in, b_ff_l, layer):
    c = HEAD_COLS
    w_all = _regroup_cast(jnp.swapaxes(w_in, 1, 2), layer, _pick(w_in.shape[1], (512, 256)))
    w_ff = jnp.pad(w_in[layer, :, 7 * c:7 * c + FOX_HEADS], ((0, 0), (0, LANES - FOX_HEADS))).astype(BF16)
    b_ff = jnp.pad(b_ff_l.astype(F32), (0, LANES - FOX_HEADS)).reshape(1, LANES)
    return w_all, w_ff, b_ff


def kernel(x_prompt, x_sample, mem_prompt, state_ret, cache_fox_k, cache_fox_v, cache_fox_logf,
           cache_mem_k, cache_mem_v, page_table, norm_mix, w_in, b_fox_f, gn_ret, norm_mem, w_mem_kv,
           w_br_ret, w_br_fox, w_br_mem, w_o, norm_mlp, w_up, w_down, norm_final):
    B, L, D = x_prompt.shape
    Bd, T, _ = x_sample.shape
    depth = w_in.shape[0]
    n_pool, ps = cache_fox_k.shape[1], cache_fox_k.shape[2]
    n_pages = page_table.shape[1]
    M = mem_prompt.shape[1]
    H = FOX_HEADS
    past_len = n_pages * ps
    pos_p = jnp.arange(L, dtype=jnp.int32)
    pos_s = past_len + jnp.arange(T, dtype=jnp.int32)

    col_kv = COL_GATES + pl.cdiv(N_BRANCH * D, HEAD_COLS)
    n_zb = col_kv * HEAD_COLS
    tn = _pick(n_zb, (2048, 1024))
    tm_p = _pick(B * L, (1024, 512, 256))
    tm_kv = _pick(B * L, (512, 256))
    C_ret = _pick(L, (256, 128))
    tq_fox = _pick(L, (512, 256, 128))
    tq_mem = _pick(L, (512, 256, 128))
    tm_merge = _pick(B * L, (256,))
    tm_mlp = _pick(B * L, (1024, 512, 256))
    tf_mlp = _pick(w_up.shape[2], (512,))
    G = _pick(n_pages, (8, 4, 2))

    cache_k = cache_fox_k.reshape(depth, n_pool, ps * H, FOX_HD)
    cache_v = cache_fox_v.reshape(depth, n_pool, ps * H, FOX_HD)
    cache_lf = cache_fox_logf.reshape(depth, n_pool, ps * H // LANES, LANES)

    xp = x_prompt.reshape(B * L, D)
    xs = x_sample.reshape(Bd * T, D)
    outs = {k: [] for k in ("sp", "ss", "kp", "vp", "fp", "ks", "vs", "fs", "mk", "mv")}
    for l in range(depth):
        last = l == depth - 1
        w_all, w_ff, b_ff = _split_w_in(w_in, b_fox_f[l], l)
        w_ret = w_br_ret[l].astype(BF16)
        w_fox = w_br_fox[l].astype(BF16)
        w_mem = w_br_mem[l].astype(BF16)
        w_o_l = w_o[l].astype(BF16)
        w_up_l = w_up[l].astype(BF16)
        w_down_l = w_down[l].astype(BF16)

        zb = _norm_proj(xp, norm_mix[l], w_all, n_zb, tm_p, tn, BF16)
        fk, fv, logf = _norm_kv(xp, norm_mix[l], w_all, col_kv, tm_kv, w_ff, b_ff)
        zb3 = zb.reshape(B, L, zb.shape[1])
        o_ret, s_new = _retention(zb3, pos_p, gn_ret[l], None, C_ret, BF16)
        logf3 = logf[:, :H].reshape(B, L, H)
        ct_row = _cumsum_lanes(jnp.swapaxes(logf3, 1, 2))
        o_fox = _fox_prompt(zb3, fk.reshape(B, L, HEAD_COLS), fv.reshape(B, L, HEAD_COLS), ct_row,
                            jnp.broadcast_to(ct_row[..., None], ct_row.shape + (LANES,)), tq_fox)
        mk, mv = _norm_kv(mem_prompt.reshape(B * M, D), norm_mem[l], w_mem_kv[l].astype(BF16), 0,
                          _pick(B * M, (512, 256)))
        o_mem = _mem_attend(zb3, mk.reshape(B, M, HEAD_COLS), mv.reshape(B, M, HEAD_COLS), tq_mem, BF16)
        xp = _merge(xp, o_ret.reshape(B * L, HEAD_COLS), o_fox.reshape(B * L, HEAD_COLS),
                    o_mem.reshape(B * L, HEAD_COLS), zb, w_ret, w_fox, w_mem, w_o_l, tm_merge)
        zs = _norm_proj(xs, norm_mix[l], w_all, n_zb, Bd * T, tn, F32)
        fk_s, fv_s, logf_s = _norm_kv(xs, norm_mix[l], w_all, col_kv, Bd * T, w_ff, b_ff)
        logf3_s = logf_s[:, :H].reshape(Bd, T, H)
        fox_in = (zs[:, COL_FQ * HEAD_COLS:(COL_FQ + 1) * HEAD_COLS].reshape(Bd, T * H, FOX_HD),
                  fk_s.reshape(Bd, T * H, FOX_HD), fv_s.reshape(Bd, T * H, FOX_HD),
                  jnp.pad(logf3_s.reshape(Bd, 1, T * H), ((0, 0), (0, 0), (0, LANES - T * H))),
                  cache_k, cache_v, cache_lf, page_table, l)
        shared = _shared_tiling(B * L // tm_mlp * (w_up.shape[2] // tf_mlp), Bd, n_pages)
        if shared is None:
            xp = _mlp(xp, norm_mlp[l], w_up_l, w_down_l, norm_final, tm_mlp, tf_mlp, last)
            o_fox_s = _fox_sample(*fox_in, G)
        else:
            xp, o_fox_s = _mlp_fox(xp, norm_mlp[l], w_up_l, w_down_l, norm_final, last, tm_mlp, tf_mlp,
                                   *fox_in, *shared)
        outs["sp"].append(s_new)
        outs["kp"].append(fk.reshape(B, L, H, FOX_HD))
        outs["vp"].append(fv.reshape(B, L, H, FOX_HD))
        outs["fp"].append(logf3)
        outs["mk"].append(mk.reshape(B, M, MEM_HEADS, MEM_HD))
        outs["mv"].append(mv.reshape(B, M, MEM_HEADS, MEM_HD))

        zs3 = zs.reshape(Bd, T, zs.shape[1])
        o_ret, s_new = _retention(zs3, pos_s, gn_ret[l], state_ret[l], T, F32)
        o_mem = _mem_attend(zs3, cache_mem_k[l].reshape(Bd, M, HEAD_COLS),
                            cache_mem_v[l].reshape(Bd, M, HEAD_COLS), T, F32)
        xs = _merge(xs, o_ret.reshape(Bd * T, HEAD_COLS), o_fox_s.reshape(Bd * T, HEAD_COLS),
                    o_mem.reshape(Bd * T, HEAD_COLS), zs, w_ret, w_fox, w_mem, w_o_l, Bd * T)
        xs = _mlp(xs, norm_mlp[l], w_up_l, w_down_l, norm_final, Bd * T, tf_mlp, last)
        outs["ss"].append(s_new)
        outs["ks"].append(fk_s.reshape(Bd, T, H, FOX_HD))
        outs["vs"].append(fv_s.reshape(Bd, T, H, FOX_HD))
        outs["fs"].append(logf3_s)

    st = lambda k: jnp.stack(outs[k])
    return (xp.reshape(B, L, D), xs.reshape(Bd, T, D), st("sp"), st("ss"), st("kp"), st("vp"), st("fp"),
            st("ks"), st("vs"), st("fs"), st("mk"), st("mv"))
```

```python
import functools
import math

import jax
import jax.numpy as jnp
from jax import lax
from jax.experimental import pallas as pl
from jax.experimental.pallas import tpu as pltpu

F32 = jnp.float32
BF16 = jnp.bfloat16

RET_HEADS = 8
RET_DK = 128
RET_DV = 128
FOX_HEADS = 8
FOX_HD = 128
MEM_HEADS = 4
MEM_HD = 256
N_BRANCH = 3
ROPE_BASE = 10000.0
RMS_EPS = 1e-6
GN_EPS = 1e-5

HEAD_COLS = 1024
LANES = 128
SUBLANES = 8
VMEM_LIMIT = 56 * 1024 * 1024
LOG2E = math.log2(math.e)
PAGE_RING_SLOTS = 3
MLP_FOX_RING_SLOTS = 2

COL_FQ = 4
COL_MQ = 5
COL_GATES = 6


def _dot(a, b):
    return jnp.dot(a, b, preferred_element_type=F32)


def _dot_nt(a, b):
    return lax.dot_general(a, b, (((1,), (1,)), ((), ())), preferred_element_type=F32)


def _dot_tn(a, b):
    return lax.dot_general(a, b, (((0,), (0,)), ((), ())), preferred_element_type=F32)


def _params(sem):
    return pltpu.CompilerParams(dimension_semantics=sem, vmem_limit_bytes=VMEM_LIMIT)


def _rms_normed(x, g):
    ms = jnp.mean(x * x, axis=-1, keepdims=True)
    return x * lax.rsqrt(ms + RMS_EPS) * g


def _log_sigmoid(x):
    return jnp.minimum(x, 0.0) - jnp.log1p(jnp.exp(-jnp.abs(x)))


def _norm_proj_kernel(x_ref, g_ref, w_ref, o_ref, hn_ref):
    @pl.when(pl.program_id(1) == 0)
    def _():
        hn_ref[...] = _rms_normed(x_ref[...], g_ref[...]).astype(BF16)

    o_ref[...] = _dot(hn_ref[...], w_ref[...]).astype(o_ref.dtype)


def _norm_proj(x2d, g, w, n_cols, tm, tn, out_dtype):
    T, D = x2d.shape
    assert n_cols % tn == 0
    return pl.pallas_call(
        _norm_proj_kernel,
        grid=(T // tm, n_cols // tn),
        in_specs=[pl.BlockSpec((tm, D), lambda i, j: (i, 0)),
                  pl.BlockSpec((1, D), lambda i, j: (0, 0)),
                  pl.BlockSpec((D, tn), lambda i, j: (0, j))],
        out_specs=pl.BlockSpec((tm, tn), lambda i, j: (i, j)),
        out_shape=jax.ShapeDtypeStruct((T, n_cols), out_dtype),
        scratch_shapes=[pltpu.VMEM((tm, D), BF16)],
        compiler_params=_params(("parallel", "arbitrary")),
        name="norm_proj",
    )(x2d, g.reshape(1, D), w)


def _regroup_cast_kernel(a_ref, b_ref, o_ref, *, n_lead, n_gate_blocks, shift):
    j = pl.program_id(1)
    shifted = (j >= n_lead) & (j < n_lead + 1 + n_gate_blocks)

    @pl.when(jnp.logical_not(shifted))
    def _():
        o_ref[...] = a_ref[...].T.astype(BF16)

    @pl.when(shifted)
    def _():
        x = jnp.concatenate([a_ref[shift:, :], b_ref[:shift, :]], axis=0)
        o_ref[...] = x.T.astype(BF16)


def _regroup_cast(w_t, layer, tr):
    _, N, D = w_t.shape
    n_lead = 5
    n_gate_blocks = pl.cdiv(N_BRANCH * D, HEAD_COLS)
    n_blocks = n_lead + 1 + n_gate_blocks + 2
    last = pl.cdiv(N, HEAD_COLS) - 1

    def src(j):
        return jnp.where(j < n_lead, j, jnp.where(j < n_lead + 1 + n_gate_blocks, j + 2, j - 1 - n_gate_blocks))

    def src_next(j):
        shifted = (j >= n_lead) & (j < n_lead + 1 + n_gate_blocks)
        return jnp.where(shifted, jnp.minimum(src(j) + 1, last), n_lead + 3)

    return pl.pallas_call(
        functools.partial(_regroup_cast_kernel, n_lead=n_lead, n_gate_blocks=n_gate_blocks, shift=FOX_HEADS),
        grid=(D // tr, n_blocks),
        in_specs=[pl.BlockSpec((None, HEAD_COLS, tr), lambda r, j: (layer, src(j), r)),
                  pl.BlockSpec((None, HEAD_COLS, tr), lambda r, j: (layer, src_next(j), r))],
        out_specs=pl.BlockSpec((tr, HEAD_COLS), lambda r, j: (r, j)),
        out_shape=jax.ShapeDtypeStruct((D, n_blocks * HEAD_COLS), BF16),
        compiler_params=_params(("parallel", "parallel")),
        name="regroup_cast",
    )(w_t, w_t)


def _norm_kv_kernel(*refs, has_ff):
    if has_ff:
        x_ref, g_ref, wk_ref, wv_ref, wff_ref, bff_ref, k_ref, v_ref, logf_ref = refs
    else:
        x_ref, g_ref, wk_ref, wv_ref, k_ref, v_ref = refs
    hn = _rms_normed(x_ref[...], g_ref[...]).astype(BF16)
    k_ref[...] = _dot(hn, wk_ref[...])
    v_ref[...] = _dot(hn, wv_ref[...])
    if has_ff:
        logf_ref[...] = _log_sigmoid(_dot(hn, wff_ref[...]) + bff_ref[...])


def _norm_kv(x2d, g, w, col_block, tm, w_ff=None, b_ff=None):
    T, D = x2d.shape
    has_ff = w_ff is not None
    row = lambda i: (i, 0)
    const = lambda i: (0, 0)
    in_specs = [pl.BlockSpec((tm, D), row), pl.BlockSpec((1, D), const),
                pl.BlockSpec((D, HEAD_COLS), lambda i: (0, col_block), pipeline_mode=pl.Buffered(1)),
                pl.BlockSpec((D, HEAD_COLS), lambda i: (0, col_block + 1), pipeline_mode=pl.Buffered(1))]
    args = [x2d, g.reshape(1, D), w, w]
    out_specs = [pl.BlockSpec((tm, HEAD_COLS), row), pl.BlockSpec((tm, HEAD_COLS), row)]
    out_shape = [jax.ShapeDtypeStruct((T, HEAD_COLS), F32), jax.ShapeDtypeStruct((T, HEAD_COLS), F32)]
    if has_ff:
        in_specs += [pl.BlockSpec((D, LANES), const), pl.BlockSpec((1, LANES), const)]
        args += [w_ff, b_ff]
        out_specs.append(pl.BlockSpec((tm, LANES), row))
        out_shape.append(jax.ShapeDtypeStruct((T, LANES), F32))
    return pl.pallas_call(
        functools.partial(_norm_kv_kernel, has_ff=has_ff),
        grid=(T // tm,),
        in_specs=in_specs,
        out_specs=out_specs,
        out_shape=out_shape,
        compiler_params=_params(("parallel",)),
        name="norm_kv",
    )(*args)


def _retention_kernel(*refs, T, Cp, has_s0):
    q_ref, k_ref, v_ref, g_ref, cos_ref, sin_ref, inner_ref, qd_ref, kd_ref, cd_ref, gn_ref = refs[:11]
    pos = 11
    if has_s0:
        s0_ref = refs[pos]
        pos += 1
    o_ref, s_out_ref, s_ref = refs[pos:pos + 3]
    c = pl.program_id(1)
    nc = pl.num_programs(1)

    @pl.when(c == 0)
    def _():
        if has_s0:
            s_ref[...] = s0_ref[0]
        else:
            s_ref[...] = jnp.zeros_like(s_ref)

    cos = cos_ref[...]
    sin = sin_ref[...]

    def pad(t):
        if T == Cp:
            return t
        return jnp.concatenate([t, jnp.zeros((Cp - T, t.shape[1]), t.dtype)], axis=0)

    for h in range(RET_HEADS):
        sl = slice(h * RET_DK, (h + 1) * RET_DK)
        q = pad(q_ref[0, :, sl].astype(F32))
        k = pad(k_ref[0, :, sl].astype(F32))
        v = pad(v_ref[0, :, sl].astype(F32)).astype(BF16)
        q = q * cos + pltpu.roll(q, RET_DK // 2, 1) * sin
        k = (k * cos + pltpu.roll(k, RET_DK // 2, 1) * sin) * (RET_DK ** -0.5)
        qb = q.astype(BF16)
        att = _dot_nt(qb, k.astype(BF16)) * inner_ref[h]
        s_prev = s_ref[h]
        o = _dot(att.astype(BF16), v) + _dot(qb, s_prev.astype(BF16)) * qd_ref[h]
        s_ref[h] = s_prev * cd_ref[h] + _dot_tn((k * kd_ref[h]).astype(BF16), v)
        mu = jnp.mean(o, axis=-1, keepdims=True)
        d = o - mu
        var = jnp.mean(d * d, axis=-1, keepdims=True)
        y = d * lax.rsqrt(var + GN_EPS) * gn_ref[:, sl]
        gate = g_ref[0, :, sl].astype(F32)
        gate = gate * jax.nn.sigmoid(gate)
        o_ref[0, :, sl] = (gate * y[:T]).astype(o_ref.dtype)

    @pl.when(c == nc - 1)
    def _():
        s_out_ref[0] = s_ref[...]


def _retention_tables(C, Cp):
    H = RET_HEADS
    lg = jnp.log1p(-jnp.exp2(-5.0 - jnp.arange(H, dtype=F32)))
    i = jnp.arange(C, dtype=F32)
    diff = i[:, None] - i[None, :]
    inner = jnp.where((diff >= 0)[None], jnp.exp(jnp.maximum(diff, 0.0)[None] * lg[:, None, None]), 0.0)
    qd = jnp.exp((i[None, :] + 1.0) * lg[:, None])
    kd = jnp.exp((C - 1.0 - i)[None, :] * lg[:, None])
    cd = jnp.exp(C * lg)
    inner = jnp.pad(inner, ((0, 0), (0, Cp - C), (0, Cp - C)))
    qd = jnp.broadcast_to(jnp.pad(qd, ((0, 0), (0, Cp - C)))[:, :, None], (H, Cp, LANES))
    kd = jnp.broadcast_to(jnp.pad(kd, ((0, 0), (0, Cp - C)))[:, :, None], (H, Cp, LANES))
    cd = jnp.broadcast_to(cd[:, None, None], (H, 1, LANES))
    return inner, qd, kd, cd


def _rope_tables(pos, rows):
    half = RET_DK // 2
    inv = ROPE_BASE ** (-jnp.arange(half, dtype=F32) / half)
    ang = pos.astype(F32)[:, None] * inv[None, :]
    cos = jnp.cos(ang)
    sin = jnp.sin(ang)
    cos_full = jnp.concatenate([cos, cos], axis=-1)
    sin_signed = jnp.concatenate([-sin, sin], axis=-1)
    padn = rows - pos.shape[0]
    return jnp.pad(cos_full, ((0, padn), (0, 0))), jnp.pad(sin_signed, ((0, padn), (0, 0)))


def _retention(z, pos, gn, s0, C, out_dtype):
    B, L, _ = z.shape
    n = L // C
    Cp = max(C, LANES)
    inner, qd, kd, cd = _retention_tables(C, Cp)
    cos, sin = _rope_tables(pos, n * Cp)
    has_s0 = s0 is not None
    H = RET_HEADS

    def seg(k):
        return pl.BlockSpec((1, C, HEAD_COLS), lambda b, c, k=k: (b, c, k))

    const3 = lambda b, c: (0, 0, 0)
    in_specs = [seg(0), seg(1), seg(2), seg(3),
                pl.BlockSpec((Cp, LANES), lambda b, c: (c, 0)),
                pl.BlockSpec((Cp, LANES), lambda b, c: (c, 0)),
                pl.BlockSpec((H, Cp, Cp), const3),
                pl.BlockSpec((H, Cp, LANES), const3),
                pl.BlockSpec((H, Cp, LANES), const3),
                pl.BlockSpec((H, 1, LANES), const3),
                pl.BlockSpec((1, HEAD_COLS), lambda b, c: (0, 0))]
    args = [z, z, z, z, cos, sin, inner, qd, kd, cd, gn.reshape(1, HEAD_COLS)]
    if has_s0:
        in_specs.append(pl.BlockSpec((1, H, RET_DK, RET_DV), lambda b, c: (b, 0, 0, 0)))
        args.append(s0)
    return pl.pallas_call(
        functools.partial(_retention_kernel, T=C, Cp=Cp, has_s0=has_s0),
        grid=(B, n),
        in_specs=in_specs,
        out_specs=[pl.BlockSpec((1, C, HEAD_COLS), lambda b, c: (b, c, 0)),
                   pl.BlockSpec((1, H, RET_DK, RET_DV), lambda b, c: (b, 0, 0, 0))],
        out_shape=[jax.ShapeDtypeStruct((B, L, HEAD_COLS), out_dtype),
                   jax.ShapeDtypeStruct((B, H, RET_DK, RET_DV), F32)],
        scratch_shapes=[pltpu.VMEM((H, RET_DK, RET_DV), F32)],
        compiler_params=_params(("parallel", "arbitrary")),
        name="retention",
    )(*args)


def _split3(x):
    hi = x.astype(BF16).astype(F32)
    r = x - hi
    mid = r.astype(BF16).astype(F32)
    lo = (r - mid).astype(BF16).astype(F32)
    return hi, mid, lo


def _lane_scan(x, stride, strict_suffix):
    n = x.shape[0]
    j = lax.broadcasted_iota(jnp.int32, (LANES, LANES), 0)
    s = lax.broadcasted_iota(jnp.int32, (LANES, LANES), 1)
    same = ((j - s) & (stride - 1)) == 0
    tri = jnp.logical_and(same, (j > s) if strict_suffix else (j <= s))
    rhs = jnp.concatenate([jnp.where(tri, 1.0, 0.0), jnp.where(same, 1.0, 0.0)], axis=1).astype(F32)
    hi, mid, lo = _split3(x)
    y = _dot(jnp.concatenate([hi, mid, lo], axis=0), rhs)
    y = y[0:n] + y[n:2 * n] + y[2 * n:3 * n]
    return y[:, :LANES], y[:, LANES:]


def _cumsum_kernel(x_ref, o_ref, *, n_chunks):
    carry = jnp.zeros((SUBLANES, LANES), F32)
    for c in range(n_chunks):
        sl = slice(c * LANES, (c + 1) * LANES)
        scan, tot = _lane_scan(x_ref[0, :, sl], 1, strict_suffix=False)
        o_ref[0, :, sl] = scan + carry
        carry = carry + tot


def _cumsum_lanes(x):
    B, H, N = x.shape
    return pl.pallas_call(
        functools.partial(_cumsum_kernel, n_chunks=N // LANES),
        grid=(B,),
        in_specs=[pl.BlockSpec((1, H, N), lambda b: (b, 0, 0))],
        out_specs=pl.BlockSpec((1, H, N), lambda b: (b, 0, 0)),
        out_shape=jax.ShapeDtypeStruct((B, H, N), F32),
        compiler_params=_params(("parallel",)),
        name="logf_cumsum",
    )(x)


def _fox_prompt_kernel(q_ref, k_ref, v_ref, cq_ref, ck_ref, o_ref, m_ref, l_ref, acc_ref, *, tq):
    qi = pl.program_id(1)
    ki = pl.program_id(2)
    kscale = (FOX_HD ** -0.5) * LOG2E

    @pl.when(ki == 0)
    def _():
        m_ref[...] = jnp.full_like(m_ref, -jnp.inf)
        l_ref[...] = jnp.zeros_like(l_ref)
        acc_ref[...] = jnp.zeros_like(acc_ref)

    def scores(h):
        sl = slice(h * FOX_HD, (h + 1) * FOX_HD)
        k = (k_ref[0, :, sl] * kscale).astype(BF16)
        return _dot_nt(k, q_ref[0, :, sl])

    def step(diagonal):
        if diagonal:
            key = lax.broadcasted_iota(jnp.int32, (tq, tq), 0)
            qry = lax.broadcasted_iota(jnp.int32, (tq, tq), 1)
            causal = key <= qry
        s_next = scores(0)
        for h in range(FOX_HEADS):
            sl = slice(h * FOX_HD, (h + 1) * FOX_HD)
            s = s_next
            if h + 1 < FOX_HEADS:
                s_next = scores(h + 1)
            ck = ck_ref[0, h] * LOG2E
            t = jnp.concatenate([s[:, c * LANES:(c + 1) * LANES] - ck for c in range(tq // LANES)], axis=1)
            if diagonal:
                t = jnp.where(causal, t, -jnp.inf)
            cq = cq_ref[0, h:h + 1, :] * LOG2E
            m_prev = m_ref[h]
            m_new = jnp.maximum(m_prev, jnp.max(t, axis=0, keepdims=True) + cq)
            corr = jnp.exp2(m_prev - m_new)
            p = jnp.exp2(t - (m_new - cq))
            l_ref[h] = l_ref[h] * corr + jnp.sum(p, axis=0, keepdims=True)
            v = v_ref[0, :, sl].astype(BF16)
            acc_ref[h] = acc_ref[h] * corr + _dot_tn(v, p.astype(BF16))
            m_ref[h] = m_new

    @pl.when(ki < qi)
    def _():
        step(False)

    @pl.when(ki == qi)
    def _():
        step(True)
        for h in range(FOX_HEADS):
            sl = slice(h * FOX_HD, (h + 1) * FOX_HD)
            o_ref[0, :, sl] = (acc_ref[h] / l_ref[h]).T.astype(o_ref.dtype)


def _fox_prompt(zb, fk, fv, ct_row, ct_lanes, tq):
    B, L, _ = fk.shape
    n = L // tq
    kv_map = lambda b, qi, ki: (b, jnp.minimum(ki, qi), 0)
    return pl.pallas_call(
        functools.partial(_fox_prompt_kernel, tq=tq),
        grid=(B, n, n),
        in_specs=[pl.BlockSpec((1, tq, HEAD_COLS), lambda b, qi, ki: (b, qi, COL_FQ)),
                  pl.BlockSpec((1, tq, HEAD_COLS), kv_map),
                  pl.BlockSpec((1, tq, HEAD_COLS), kv_map),
                  pl.BlockSpec((1, FOX_HEADS, tq), lambda b, qi, ki: (b, 0, qi)),
                  pl.BlockSpec((1, FOX_HEADS, tq, LANES), lambda b, qi, ki: (b, 0, jnp.minimum(ki, qi), 0))],
        out_specs=pl.BlockSpec((1, tq, HEAD_COLS), lambda b, qi, ki: (b, qi, 0)),
        out_shape=jax.ShapeDtypeStruct((B, L, HEAD_COLS), BF16),
        scratch_shapes=[pltpu.VMEM((FOX_HEADS, 1, tq), F32),
                        pltpu.VMEM((FOX_HEADS, 1, tq), F32),
                        pltpu.VMEM((FOX_HEADS, FOX_HD, tq), F32)],
        compiler_params=_params(("parallel", "parallel", "arbitrary")),
        name="fox_prompt",
    )(zb, fk, fv, ct_row, ct_lanes)


def _mem_attend_kernel(q_ref, k_ref, v_ref, o_ref):
    scale = MEM_HD ** -0.5
    for h in range(MEM_HEADS):
        sl = slice(h * MEM_HD, (h + 1) * MEM_HD)
        q = q_ref[0, :, sl].astype(BF16)
        k = (k_ref[0, :, sl] * scale).astype(BF16)
        v = v_ref[0, :, sl].astype(BF16)
        s = _dot_nt(q, k)
        p = jnp.exp(s - jnp.max(s, axis=1, keepdims=True))
        o = _dot(p.astype(BF16), v) / jnp.sum(p, axis=1, keepdims=True)
        o_ref[0, :, sl] = o.astype(o_ref.dtype)


def _mem_attend(zb, mk, mv, tq, out_dtype):
    B, L, _ = zb.shape
    M = mk.shape[1]
    return pl.pallas_call(
        _mem_attend_kernel,
        grid=(B, L // tq),
        in_specs=[pl.BlockSpec((1, tq, HEAD_COLS), lambda b, i: (b, i, COL_MQ)),
                  pl.BlockSpec((1, M, HEAD_COLS), lambda b, i: (b, 0, 0)),
                  pl.BlockSpec((1, M, HEAD_COLS), lambda b, i: (b, 0, 0))],
        out_specs=pl.BlockSpec((1, tq, HEAD_COLS), lambda b, i: (b, i, 0)),
        out_shape=jax.ShapeDtypeStruct((B, L, HEAD_COLS), out_dtype),
        compiler_params=_params(("parallel", "parallel")),
        name="mem_attend",
    )(zb, mk, mv)


def _merge_kernel(x_ref, oret_ref, ofox_ref, omem_ref, gate_ref, wr_ref, wf_ref, wm_ref, wo_ref, *refs, D):
    if len(refs) == 1:
        (o_ref,) = refs
    else:
        wu_in, wd_in, o_ref, wu_out, wd_out = refs
        wu_out[...] = wu_in[...].astype(BF16)
        wd_out[...] = wd_in[...].astype(BF16)

    def branch(k, o_b, w_ref):
        g = jax.nn.sigmoid(gate_ref[:, k * D:(k + 1) * D].astype(F32))
        return g * _dot(o_b[...].astype(BF16), w_ref[...])

    merged = branch(0, oret_ref, wr_ref) + branch(1, ofox_ref, wf_ref) + branch(2, omem_ref, wm_ref)
    o_ref[...] = x_ref[...] + _dot(merged.astype(BF16), wo_ref[...])


def _merge(x2d, o_ret, o_fox, o_mem, zb, w_ret, w_fox, w_mem, w_o, tm, mlp_weights=None):
    T, D = x2d.shape
    gate_blk = COL_GATES * HEAD_COLS // (N_BRANCH * D)
    assert gate_blk * N_BRANCH * D == COL_GATES * HEAD_COLS
    row = lambda i: (i, 0)
    const = lambda i: (0, 0)
    resident = functools.partial(pl.BlockSpec, index_map=const, pipeline_mode=pl.Buffered(1))
    n = T // tm
    in_specs = [pl.BlockSpec((tm, D), row),
                pl.BlockSpec((tm, HEAD_COLS), row),
                pl.BlockSpec((tm, HEAD_COLS), row),
                pl.BlockSpec((tm, HEAD_COLS), row),
                pl.BlockSpec((tm, N_BRANCH * D), lambda i: (i, gate_blk)),
                resident((HEAD_COLS, D)),
                resident((HEAD_COLS, D)),
                resident((HEAD_COLS, D)),
                resident((D, D))]
    args = [x2d, o_ret, o_fox, o_mem, zb, w_ret, w_fox, w_mem, w_o]
    out_specs = [pl.BlockSpec((tm, D), row)]
    out_shape = [jax.ShapeDtypeStruct((T, D), F32)]
    if mlp_weights is not None:
        for w in mlp_weights:
            slab = (w.shape[0] // n, w.shape[1])
            assert slab[0] * n == w.shape[0]
            in_specs.append(pl.BlockSpec(slab, row))
            out_specs.append(pl.BlockSpec(slab, row))
            out_shape.append(jax.ShapeDtypeStruct(w.shape, BF16))
            args.append(w)
    out = pl.pallas_call(
        functools.partial(_merge_kernel, D=D),
        grid=(n,),
        in_specs=in_specs,
        out_specs=out_specs,
        out_shape=out_shape,
        compiler_params=_params(("parallel",)),
        name="merge",
    )(*args)
    return out if mlp_weights is not None else out[0]


def _mlp_kernel(x_ref, g_ref, wu_ref, wd_ref, gf_ref, o_ref, hn_ref, *, final):
    f = pl.program_id(1)
    nf = pl.num_programs(1)

    @pl.when(f == 0)
    def _():
        x = x_ref[...]
        hn_ref[...] = _rms_normed(x, g_ref[...]).astype(BF16)
        o_ref[...] = x

    u = jnp.maximum(_dot(hn_ref[...], wu_ref[...]), 0.0)
    o_ref[...] += _dot((u * u).astype(BF16), wd_ref[...])

    if final:
        @pl.when(f == nf - 1)
        def _():
            o_ref[...] = _rms_normed(o_ref[...], gf_ref[...])


def _mlp(x2d, g, w_up, w_down, g_final, tm, tf, final):
    T, D = x2d.shape
    Fd = w_up.shape[1]
    return pl.pallas_call(
        functools.partial(_mlp_kernel, final=final),
        grid=(T // tm, Fd // tf),
        in_specs=[pl.BlockSpec((tm, D), lambda i, f: (i, 0)),
                  pl.BlockSpec((1, D), lambda i, f: (0, 0)),
                  pl.BlockSpec((D, tf), lambda i, f: (0, f)),
                  pl.BlockSpec((tf, D), lambda i, f: (f, 0)),
                  pl.BlockSpec((1, D), lambda i, f: (0, 0))],
        out_specs=pl.BlockSpec((tm, D), lambda i, f: (i, 0)),
        out_shape=jax.ShapeDtypeStruct((T, D), F32),
        scratch_shapes=[pltpu.VMEM((tm, D), BF16)],
        compiler_params=_params(("parallel", "arbitrary")),
        name="mlp",
    )(x2d, g.reshape(1, D), w_up, w_down, g_final.reshape(1, D))


def _fox_new_tokens(q, kn, vn, lfn, qall_ref, ptcol_ref, m_ref, l_ref, acc_ref, carry_ref):
    H = FOX_HEADS
    R = q.shape[0]
    row = lax.broadcasted_iota(jnp.int32, (R, LANES), 0)
    lane = lax.broadcasted_iota(jnp.int32, (R, LANES), 1)
    qall = (q * (FOX_HD ** -0.5)).astype(BF16)
    qall_ref[...] = qall
    ptrow = _lane_scan(jnp.broadcast_to(lfn, (SUBLANES, LANES)), H, strict_suffix=False)[0][0:1]
    ptcol = jnp.sum(jnp.where(lane == row, ptrow, 0.0), axis=1, keepdims=True)
    ptcol_ref[...] = ptcol
    zpad = jnp.zeros((LANES - R, FOX_HD), F32)
    kn = jnp.concatenate([kn, zpad], axis=0).astype(BF16)
    vn = jnp.concatenate([vn, zpad], axis=0).astype(BF16)
    s = _dot_nt(qall, kn) + (ptcol - ptrow)
    causal = jnp.logical_and(((lane - row) & (H - 1)) == 0, (lane // H) <= (row // H))
    s = jnp.where(causal, s, -jnp.inf)
    m = jnp.max(s, axis=1, keepdims=True)
    p = jnp.exp(s - m)
    m_ref[...] = m
    l_ref[...] = jnp.sum(p, axis=1, keepdims=True)
    acc_ref[...] = _dot(p.astype(BF16), vn)
    carry_ref[...] = jnp.zeros_like(carry_ref)


def _fox_pages(k_pages, v_pages, lf_pages, qall_ref, ptcol_ref, m_ref, l_ref, acc_ref, carry_ref):
    H = FOX_HEADS
    G = len(k_pages)
    R = qall_ref.shape[0]
    row = lax.broadcasted_iota(jnp.int32, (R, LANES), 0)
    lane = lax.broadcasted_iota(jnp.int32, (R, LANES), 1)
    same_head = ((lane - row) & (H - 1)) == 0
    suf, rowtot = _lane_scan(jnp.concatenate(lf_pages, axis=0), H, strict_suffix=True)
    sub = lax.broadcasted_iota(jnp.int32, (SUBLANES, LANES), 0)
    qall = qall_ref[...]
    ptcol = ptcol_ref[...]
    carry = carry_ref[0:1, :]
    tiles = []
    tmax = None
    for i in range(G):
        pre = rowtot[i * SUBLANES:(i + 1) * SUBLANES]
        for sh in (1, 2, 4):
            pre = pre + jnp.where(sub >= sh, pltpu.roll(pre, sh, 0), 0.0)
        page_tot = pre[SUBLANES - 1:SUBLANES]
        later = suf[i * SUBLANES:(i + 1) * SUBLANES] + (page_tot - pre) + carry
        carry = carry + page_tot
        s2 = _dot(qall, k_pages[i][...].T.astype(BF16))
        for r in range(SUBLANES):
            t = jnp.where(same_head, s2[:, r * LANES:(r + 1) * LANES] + later[r:r + 1, :], -jnp.inf)
            tiles.append(t)
            tmax = t if tmax is None else jnp.maximum(tmax, t)
    carry_ref[...] = jnp.broadcast_to(carry, carry_ref.shape)
    m_prev = m_ref[...]
    m_new = jnp.maximum(m_prev, jnp.max(tmax, axis=1, keepdims=True) + ptcol)
    shift = m_new - ptcol
    corr = jnp.exp(m_prev - m_new)
    psum = jnp.zeros((R, LANES), F32)
    pv = jnp.zeros((R, FOX_HD), F32)
    for i in range(G):
        p = [jnp.exp(t - shift) for t in tiles[i * SUBLANES:(i + 1) * SUBLANES]]
        for pt in p:
            psum = psum + pt
        pv = pv + _dot(jnp.concatenate(p, axis=1).astype(BF16), v_pages[i][...].astype(BF16))
    l_ref[...] = l_ref[...] * corr + jnp.sum(psum, axis=1, keepdims=True)
    acc_ref[...] = acc_ref[...] * corr + pv
    m_ref[...] = m_new


def _page_copies(pt_ref, kc_ref, vc_ref, kbuf, vbuf, sem, u, slot, G, NG, layer):
    n_pages = NG * G
    bb = lax.div(u, NG)
    gg = lax.rem(u, NG)
    out = []
    for i in range(G):
        page = pt_ref[bb, n_pages - 1 - (gg * G + i)]
        out.append(pltpu.make_async_copy(kc_ref.at[layer, page], kbuf.at[slot, i], sem.at[0, slot]))
        out.append(pltpu.make_async_copy(vc_ref.at[layer, page], vbuf.at[slot, i], sem.at[1, slot]))
    return out


def _fox_sample_kernel(pt_ref, q_ref, kn_ref, vn_ref, lfn_ref, kc_ref, vc_ref, *refs, G, NG, layer):
    lf_refs = refs[:G]
    o_ref, kbuf, vbuf, sem, qall_ref, ptcol_ref, m_ref, l_ref, acc_ref, carry_ref = refs[G:]
    g = pl.program_id(1)
    n_slots = kbuf.shape[0]
    step = pl.program_id(0) * NG + g
    n_steps = pl.num_programs(0) * NG
    page_copies = functools.partial(_page_copies, pt_ref, kc_ref, vc_ref, kbuf, vbuf, sem,
                                    G=G, NG=NG, layer=layer)

    @pl.when(step == 0)
    def _():
        for j in range(n_slots - 1):
            for c in page_copies(j, j):
                c.start()

    ahead = step + (n_slots - 1)

    @pl.when(ahead < n_steps)
    def _():
        for c in page_copies(ahead, lax.rem(ahead, n_slots)):
            c.start()

    slot = lax.rem(step, n_slots)
    for c in page_copies(step, slot):
        c.wait()
    state = (qall_ref, ptcol_ref, m_ref, l_ref, acc_ref, carry_ref)

    @pl.when(g == 0)
    def _():
        _fox_new_tokens(q_ref[0], kn_ref[0], vn_ref[0], lfn_ref[0], *state)

    _fox_pages([kbuf.at[slot, i] for i in range(G)], [vbuf.at[slot, i] for i in range(G)],
               [lf_refs[i][0, 0] for i in range(G)], *state)

    @pl.when(g == NG - 1)
    def _():
        o_ref[0] = acc_ref[...] / l_ref[...]


def _fox_sample(fq, fk, fv, lf_new, cache_k, cache_v, cache_lf, page_table, layer, G):
    B, R, _ = fq.shape
    n_pages = page_table.shape[1]
    rows = cache_k.shape[2]
    assert rows == SUBLANES * LANES and n_pages % G == 0 and R <= LANES
    NG = n_pages // G
    tok = pl.BlockSpec((1, R, FOX_HD), lambda b, g, pt: (b, 0, 0))

    def lf_spec(i):
        return pl.BlockSpec((1, 1, SUBLANES, LANES),
                            lambda b, g, pt, i=i: (layer, pt[b, n_pages - 1 - (g * G + i)], 0, 0))

    assert B * NG >= PAGE_RING_SLOTS - 1
    in_specs = ([tok, tok, tok, pl.BlockSpec((1, 1, LANES), lambda b, g, pt: (b, 0, 0)),
                 pl.BlockSpec(memory_space=pl.ANY), pl.BlockSpec(memory_space=pl.ANY)]
                + [lf_spec(i) for i in range(G)])
    grid_spec = pltpu.PrefetchScalarGridSpec(
        num_scalar_prefetch=1,
        grid=(B, NG),
        in_specs=in_specs,
        out_specs=pl.BlockSpec((1, R, FOX_HD), lambda b, g, pt: (b, 0, 0)),
        scratch_shapes=[pltpu.VMEM((PAGE_RING_SLOTS, G, rows, FOX_HD), F32),
                        pltpu.VMEM((PAGE_RING_SLOTS, G, rows, FOX_HD), F32),
                        pltpu.SemaphoreType.DMA((2, PAGE_RING_SLOTS)),
                        pltpu.VMEM((R, FOX_HD), BF16),
                        pltpu.VMEM((R, 1), F32),
                        pltpu.VMEM((R, 1), F32),
                        pltpu.VMEM((R, 1), F32),
                        pltpu.VMEM((R, FOX_HD), F32),
                        pltpu.VMEM((SUBLANES, LANES), F32)])
    return pl.pallas_call(
        functools.partial(_fox_sample_kernel, G=G, NG=NG, layer=layer),
        grid_spec=grid_spec,
        out_shape=jax.ShapeDtypeStruct((B, R, FOX_HD), F32),
        compiler_params=_params(("arbitrary", "arbitrary")),
        name="fox_sample",
    )(page_table, fq, fk, fv, lf_new, cache_k, cache_v, *([cache_lf] * G))


def _mlp_fox_kernel(pt_ref, x_hbm, g_ref, wu_ref, wd_ref, gf_ref, q_ref, kn_ref, vn_ref, lfn_ref,
                    kc_ref, vc_ref, *refs, G, NG, units, layer, final):
    n_lf = units * G
    lf_refs = refs[:n_lf]
    (o_ref, fo_ref, hn_ref, xsem, kbuf, vbuf, sem,
     qall_ref, ptcol_ref, m_ref, l_ref, acc_ref, carry_ref) = refs[n_lf:]
    i = pl.program_id(0)
    f = pl.program_id(1)
    nf = pl.num_programs(1)
    step = i * nf + f
    n_units = pl.num_programs(0) * nf * units
    n_slots = kbuf.shape[0]
    tm = o_ref.shape[0]
    part = tm // units
    state = (qall_ref, ptcol_ref, m_ref, l_ref, acc_ref, carry_ref)
    page_copies = functools.partial(_page_copies, pt_ref, kc_ref, vc_ref, kbuf, vbuf, sem,
                                    G=G, NG=NG, layer=layer)

    @pl.when(f == 0)
    def _():
        cp = pltpu.make_async_copy(x_hbm.at[pl.ds(i * tm, tm)], o_ref, xsem.at[0])
        cp.start()
        cp.wait()
        hn_ref[...] = _rms_normed(o_ref[...], g_ref[...]).astype(BF16)

    @pl.when(step == 0)
    def _():
        for u0 in range(n_slots):
            for c in page_copies(u0, u0):
                c.start()

    for j in range(units):
        u = step * units + j
        slot = lax.rem(u, n_slots)
        b = lax.div(u, NG)
        if j == 0:
            @pl.when(lax.rem(u, NG) == 0)
            def _(b=b):
                _fox_new_tokens(q_ref[b], kn_ref[b], vn_ref[b], lfn_ref[b], *state)

        for c in page_copies(u, slot):
            c.wait()
        rows = slice(j * part, (j + 1) * part)
        h = jnp.maximum(_dot(hn_ref[rows], wu_ref[...]), 0.0)
        o_ref[rows] += _dot((h * h).astype(BF16), wd_ref[...])
        _fox_pages([kbuf.at[slot, p] for p in range(G)], [vbuf.at[slot, p] for p in range(G)],
                   [lf_refs[j * G + p][0, 0] for p in range(G)], *state)
        nxt = u + n_slots

        @pl.when(nxt < n_units)
        def _(nxt=nxt, slot=slot):
            for c in page_copies(nxt, slot):
                c.start()

        if j == units - 1:
            @pl.when(lax.rem(u, NG) == NG - 1)
            def _(b=b):
                fo_ref[b] = acc_ref[...] / l_ref[...]

    if final:
        @pl.when(f == nf - 1)
        def _():
            o_ref[...] = _rms_normed(o_ref[...], gf_ref[...])


def _mlp_fox(x2d, g, w_up, w_down, g_final, final, tm, tf, fq, fk, fv, lf_new, cache_k, cache_v, cache_lf,
             page_table, layer, G, units):
    T, D = x2d.shape
    Fd = w_up.shape[1]
    B, R, _ = fq.shape
    n_pages = page_table.shape[1]
    rows = cache_k.shape[2]
    NG = n_pages // G
    ni, nf = T // tm, Fd // tf
    assert rows == SUBLANES * LANES and NG * G == n_pages and R <= LANES
    assert NG % units == 0 and ni * nf * units == B * NG and tm % units == 0
    assert B * NG >= MLP_FOX_RING_SLOTS

    def lf_spec(j, p):
        def index_map(i, f, pt):
            u = (i * nf + f) * units + j
            return (layer, pt[lax.div(u, NG), n_pages - 1 - (lax.rem(u, NG) * G + p)], 0, 0)
        return pl.BlockSpec((1, 1, SUBLANES, LANES), index_map)

    whole = lambda shape: pl.BlockSpec(shape, lambda i, f, pt: (0,) * len(shape), pipeline_mode=pl.Buffered(1))
    any_space = pl.BlockSpec(memory_space=pl.ANY)
    in_specs = ([any_space,
                 pl.BlockSpec((1, D), lambda i, f, pt: (0, 0)),
                 pl.BlockSpec((D, tf), lambda i, f, pt: (0, f)),
                 pl.BlockSpec((tf, D), lambda i, f, pt: (f, 0)),
                 pl.BlockSpec((1, D), lambda i, f, pt: (0, 0)),
                 whole((B, R, FOX_HD)), whole((B, R, FOX_HD)), whole((B, R, FOX_HD)), whole((B, 1, LANES)),
                 any_space, any_space]
                + [lf_spec(j, p) for j in range(units) for p in range(G)])
    grid_spec = pltpu.PrefetchScalarGridSpec(
        num_scalar_prefetch=1,
        grid=(ni, nf),
        in_specs=in_specs,
        out_specs=[pl.BlockSpec((tm, D), lambda i, f, pt: (i, 0)),
                   pl.BlockSpec((B, R, FOX_HD), lambda i, f, pt: (0, 0, 0))],
        scratch_shapes=[pltpu.VMEM((tm, D), BF16),
                        pltpu.SemaphoreType.DMA((1,)),
                        pltpu.VMEM((MLP_FOX_RING_SLOTS, G, rows, FOX_HD), F32),
                        pltpu.VMEM((MLP_FOX_RING_SLOTS, G, rows, FOX_HD), F32),
                        pltpu.SemaphoreType.DMA((2, MLP_FOX_RING_SLOTS)),
                        pltpu.VMEM((R, FOX_HD), BF16),
                        pltpu.VMEM((R, 1), F32),
                        pltpu.VMEM((R, 1), F32),
                        pltpu.VMEM((R, 1), F32),
                        pltpu.VMEM((R, FOX_HD), F32),
                        pltpu.VMEM((SUBLANES, LANES), F32)])
    return pl.pallas_call(
        functools.partial(_mlp_fox_kernel, G=G, NG=NG, units=units, layer=layer, final=final),
        grid_spec=grid_spec,
        out_shape=[jax.ShapeDtypeStruct((T, D), F32), jax.ShapeDtypeStruct((B, R, FOX_HD), F32)],
        compiler_params=_params(("arbitrary", "arbitrary")),
        name="mlp_fox",
    )(page_table, x2d, g.reshape(1, D), w_up, w_down, g_final.reshape(1, D), fq, fk, fv, lf_new,
      cache_k, cache_v, *([cache_lf] * (units * G)))


def _pick(n, prefs):
    for p in prefs:
        if n % p == 0:
            return p
    return n


def _shared_tiling(mlp_steps, n_rows, n_pages):
    for G in (8, 4, 2, 1):
        for units in (2, 1, 4):
            if n_pages % G == 0 and (n_pages // G) % units == 0 and n_rows * (n_pages // G) == mlp_steps * units:
                return G, units
    return None


def _split_w_in(w_in, b_ff_l, layer):
    c = HEAD_COLS
    w_all = _regroup_cast(jnp.swapaxes(w_in, 1, 2), layer, _pick(w_in.shape[1], (512, 256)))
    w_ff = jnp.pad(w_in[layer, :, 7 * c:7 * c + FOX_HEADS], ((0, 0), (0, LANES - FOX_HEADS))).astype(BF16)
    b_ff = jnp.pad(b_ff_l.astype(F32), (0, LANES - FOX_HEADS)).reshape(1, LANES)
    return w_all, w_ff, b_ff


def kernel(x_prompt, x_sample, mem_prompt, state_ret, cache_fox_k, cache_fox_v, cache_fox_logf,
           cache_mem_k, cache_mem_v, page_table, norm_mix, w_in, b_fox_f, gn_ret, norm_mem, w_mem_kv,
           w_br_ret, w_br_fox, w_br_mem, w_o, norm_mlp, w_up, w_down, norm_final):
    B, L, D = x_prompt.shape
    Bd, T, _ = x_sample.shape
    depth = w_in.shape[0]
    n_pool, ps = cache_fox_k.shape[1], cache_fox_k.shape[2]
    n_pages = page_table.shape[1]
    M = mem_prompt.shape[1]
    H = FOX_HEADS
    past_len = n_pages * ps
    pos_p = jnp.arange(L, dtype=jnp.int32)
    pos_s = past_len + jnp.arange(T, dtype=jnp.int32)

    col_kv = COL_GATES + pl.cdiv(N_BRANCH * D, HEAD_COLS)
    n_zb = col_kv * HEAD_COLS
    tn = _pick(n_zb, (2048, 1024))
    tm_p = _pick(B * L, (1024, 512, 256))
    tm_kv = _pick(B * L, (512, 256))
    C_ret = _pick(L, (256, 128))
    tq_fox = _pick(L, (512, 256, 128))
    tq_mem = _pick(L, (512, 256, 128))
    tm_merge = _pick(B * L, (256,))
    tm_mlp = _pick(B * L, (1024, 512, 256))
    tf_mlp = _pick(w_up.shape[2], (512,))
    G = _pick(n_pages, (8, 4, 2))

    cache_k = cache_fox_k.reshape(depth, n_pool, ps * H, FOX_HD)
    cache_v = cache_fox_v.reshape(depth, n_pool, ps * H, FOX_HD)
    cache_lf = cache_fox_logf.reshape(depth, n_pool, ps * H // LANES, LANES)

    xp = x_prompt.reshape(B * L, D)
    xs = x_sample.reshape(Bd * T, D)
    outs = {k: [] for k in ("sp", "ss", "kp", "vp", "fp", "ks", "vs", "fs", "mk", "mv")}
    for l in range(depth):
        last = l == depth - 1
        w_all, w_ff, b_ff = _split_w_in(w_in, b_fox_f[l], l)
        w_ret = w_br_ret[l].astype(BF16)
        w_fox = w_br_fox[l].astype(BF16)
        w_mem = w_br_mem[l].astype(BF16)
        w_o_l = w_o[l].astype(BF16)

        zb = _norm_proj(xp, norm_mix[l], w_all, n_zb, tm_p, tn, BF16)
        fk, fv, logf = _norm_kv(xp, norm_mix[l], w_all, col_kv, tm_kv, w_ff, b_ff)
        zb3 = zb.reshape(B, L, zb.shape[1])
        o_ret, s_new = _retention(zb3, pos_p, gn_ret[l], None, C_ret, BF16)
        logf3 = logf[:, :H].reshape(B, L, H)
        ct_row = _cumsum_lanes(jnp.swapaxes(logf3, 1, 2))
        o_fox = _fox_prompt(zb3, fk.reshape(B, L, HEAD_COLS), fv.reshape(B, L, HEAD_COLS), ct_row,
                            jnp.broadcast_to(ct_row[..., None], ct_row.shape + (LANES,)), tq_fox)
        mk, mv = _norm_kv(mem_prompt.reshape(B * M, D), norm_mem[l], w_mem_kv[l].astype(BF16), 0,
                          _pick(B * M, (512, 256)))
        o_mem = _mem_attend(zb3, mk.reshape(B, M, HEAD_COLS), mv.reshape(B, M, HEAD_COLS), tq_mem, BF16)
        xp, w_up_l, w_down_l = _merge(xp, o_ret.reshape(B * L, HEAD_COLS), o_fox.reshape(B * L, HEAD_COLS),
                                      o_mem.reshape(B * L, HEAD_COLS), zb, w_ret, w_fox, w_mem, w_o_l,
                                      tm_merge, mlp_weights=(w_up[l], w_down[l]))
        zs = _norm_proj(xs, norm_mix[l], w_all, n_zb, Bd * T, tn, F32)
        fk_s, fv_s, logf_s = _norm_kv(xs, norm_mix[l], w_all, col_kv, Bd * T, w_ff, b_ff)
        logf3_s = logf_s[:, :H].reshape(Bd, T, H)
        fox_in = (zs[:, COL_FQ * HEAD_COLS:(COL_FQ + 1) * HEAD_COLS].reshape(Bd, T * H, FOX_HD),
                  fk_s.reshape(Bd, T * H, FOX_HD), fv_s.reshape(Bd, T * H, FOX_HD),
                  jnp.pad(logf3_s.reshape(Bd, 1, T * H), ((0, 0), (0, 0), (0, LANES - T * H))),
                  cache_k, cache_v, cache_lf, page_table, l)
        shared = _shared_tiling(B * L // tm_mlp * (w_up.shape[2] // tf_mlp), Bd, n_pages)
        if shared is None:
            xp = _mlp(xp, norm_mlp[l], w_up_l, w_down_l, norm_final, tm_mlp, tf_mlp, last)
            o_fox_s = _fox_sample(*fox_in, G)
        else:
            xp, o_fox_s = _mlp_fox(xp, norm_mlp[l], w_up_l, w_down_l, norm_final, last, tm_mlp, tf_mlp,
                                   *fox_in, *shared)
        outs["sp"].append(s_new)
        outs["kp"].append(fk.reshape(B, L, H, FOX_HD))
        outs["vp"].append(fv.reshape(B, L, H, FOX_HD))
        outs["fp"].append(logf3)
        outs["mk"].append(mk.reshape(B, M, MEM_HEADS, MEM_HD))
        outs["mv"].append(mv.reshape(B, M, MEM_HEADS, MEM_HD))

        zs3 = zs.reshape(Bd, T, zs.shape[1])
        o_ret, s_new = _retention(zs3, pos_s, gn_ret[l], state_ret[l], T, F32)
        o_mem = _mem_attend(zs3, cache_mem_k[l].reshape(Bd, M, HEAD_COLS),
                            cache_mem_v[l].reshape(Bd, M, HEAD_COLS), T, F32)
        xs = _merge(xs, o_ret.reshape(Bd * T, HEAD_COLS), o_fox_s.reshape(Bd * T, HEAD_COLS),
                    o_mem.reshape(Bd * T, HEAD_COLS), zs, w_ret, w_fox, w_mem, w_o_l, Bd * T)
        xs = _mlp(xs, norm_mlp[l], w_up_l, w_down_l, norm_final, Bd * T, tf_mlp, last)
        outs["ss"].append(s_new)
        outs["ks"].append(fk_s.reshape(Bd, T, H, FOX_HD))
        outs["vs"].append(fv_s.reshape(Bd, T, H, FOX_HD))
        outs["fs"].append(logf3_s)

    st = lambda k: jnp.stack(outs[k])
    return (xp.reshape(B, L, D), xs.reshape(Bd, T, D), st("sp"), st("ss"), st("kp"), st("vp"), st("fp"),
            st("ks"), st("vs"), st("fs"), st("mk"), st("mv"))
```
